```python
import math
import jax, jax.numpy as jnp
from jax import lax
import numpy as np

D_MODEL = 2048
BATCH = 1
SEQ = 8192
DEPTH = 2
DEC_BATCH = 128
DEC_SEQ = 4
PAST_LEN = 8192
PAGE_SIZE = 128

HEAD_DIM = 64
BLOCK = 128
A_PATTERNS = ((128, 1), (512, 4), (2048, 16))
N_PATTERNS = len(A_PATTERNS)
A_HEADS = 8
B_HEADS = 8
B_KV_HEADS = 2
B_GROUP = B_HEADS // B_KV_HEADS
B_WINDOW = 128
C_HEADS = 8
C_KV_HEADS = 2
C_GROUP = C_HEADS // C_KV_HEADS
C_HEAD_DIM = 64
D_HEADS = 16
D_KV_HEADS = 4
D_GROUP = D_HEADS // D_KV_HEADS
D_HEAD_DIM = 64
D_FF = 5632
CONV_W = 3
LN_EPS = 1e-5
NEG_INF = -1e30
ALPHA = (2 * DEPTH) ** 0.25
BETA = (8 * DEPTH) ** -0.25
N_EVEN = (DEPTH + 1) // 2
N_ODD = DEPTH // 2
EVEN_SPLITS = (N_PATTERNS * A_HEADS * HEAD_DIM, N_PATTERNS * A_HEADS * HEAD_DIM, N_PATTERNS * A_HEADS * HEAD_DIM,
               B_HEADS * HEAD_DIM, B_KV_HEADS * HEAD_DIM, B_KV_HEADS * HEAD_DIM)
EVEN_IN = sum(EVEN_SPLITS)
EVEN_OUT = A_HEADS * HEAD_DIM + B_HEADS * HEAD_DIM
ODD_SPLITS = (C_HEADS * 2 * C_HEAD_DIM, C_KV_HEADS * 2 * C_HEAD_DIM, C_KV_HEADS * 2 * C_HEAD_DIM,
              D_HEADS * D_HEAD_DIM, D_KV_HEADS * D_HEAD_DIM, D_KV_HEADS * D_HEAD_DIM)
ODD_IN = sum(ODD_SPLITS)
ODD_OUT = C_HEADS * 2 * C_HEAD_DIM + D_HEADS * D_HEAD_DIM
V_COLUMNS = (2, 5)

kernel_name = "hybrid_dilated_sink_diff_stickbreak_decoder_step"


def _split_cols(p, sizes):
    return jnp.split(p, np.cumsum(sizes)[:-1].tolist(), axis=-1)


def _alibi_slopes(n):
    return jnp.asarray([2.0 ** (-8.0 * (h + 1) / n) for h in range(n)], dtype=jnp.float32)


def _layer_norm(x, g, b):
    xf = x.astype(jnp.float32)
    mu = jnp.mean(xf, axis=-1, keepdims=True)
    var = jnp.mean(jnp.square(xf - mu), axis=-1, keepdims=True)
    return ((xf - mu) * lax.rsqrt(var + LN_EPS) * g + b).astype(x.dtype)


def _rms_norm(x, g):
    xf = x.astype(jnp.float32)
    return (xf * lax.rsqrt(jnp.mean(jnp.square(xf), axis=-1, keepdims=True) + LN_EPS) * g).astype(x.dtype)


def _band_prompt(q, k, v, window, dilation, slopes, sinks=None):
    n, s, hk, g, dh = q.shape
    r = dilation
    reach = window // dilation
    n_prev = -(-reach // BLOCK)
    L = s // r
    nb = -(-L // BLOCK)
    Lp = nb * BLOCK

    def sub(x):
        x = jnp.moveaxis(x.reshape((n, L, r) + x.shape[2:]), 2, 1)
        return jnp.pad(x, [(0, 0), (0, 0), (0, Lp - L)] + [(0, 0)] * (x.ndim - 3))

    def band(x):
        x = jnp.pad(sub(x), [(0, 0), (0, 0), (n_prev * BLOCK, 0), (0, 0), (0, 0)])
        x = x.reshape(n, r, nb + n_prev, BLOCK, hk, dh)
        return jnp.concatenate([x[:, :, j:j + nb] for j in range(n_prev + 1)], axis=3)

    def unsub(x):
        x = x.reshape((n, r, Lp) + x.shape[4:])[:, :, :L]
        return jnp.moveaxis(x, 1, 2).reshape((n, s) + x.shape[3:])

    qb = sub(q).reshape(n, r, nb, BLOCK, hk, g, dh)
    kb, vb = band(k), band(v)
    scores = jnp.einsum("nrbqhgd,nrbkhd->nrbhgqk", qb, kb).astype(jnp.float32) * (dh ** -0.5)
    qi = jnp.arange(BLOCK)[:, None]
    kj = jnp.arange((n_prev + 1) * BLOCK)[None, :]
    dist = n_prev * BLOCK + qi - kj
    key_sub = (jnp.arange(nb) * BLOCK)[:, None, None] - n_prev * BLOCK + kj[None]
    valid = (dist >= 0) & (dist <= reach) & (key_sub >= 0)
    bias = -slopes[:, :, None, None] * (r * dist).astype(jnp.float32)
    scores = jnp.where(valid[:, None, None], scores + bias, NEG_INF)
    lse = jax.nn.logsumexp(scores, axis=-1)
    if sinks is not None:
        lse = jnp.logaddexp(lse, sinks.astype(jnp.float32)[:, :, None])
    p = jnp.exp(scores - lse[..., None]).astype(v.dtype)
    out = jnp.einsum("nrbhgqk,nrbkhd->nrbqhgd", p, vb)
    return unsub(out), unsub(jnp.moveaxis(lse, -1, 3))


def _band_sample(q, k_new, v_new, k_buf, v_buf, window, dilation, slopes, sinks=None):
    t, dh = q.shape[1], q.shape[-1]
    lb = k_buf.shape[1]
    steps = jnp.arange(window // dilation + 1)
    idx = lb + jnp.arange(t)[:, None] - dilation * steps[None, :]
    valid = idx >= 0
    sel = (idx < lb)[None, :, :, None, None]
    bi = jnp.clip(idx, 0, lb - 1)
    ni = jnp.clip(idx - lb, 0, t - 1)
    kg = jnp.where(sel, k_buf[:, bi], k_new[:, ni])
    vg = jnp.where(sel, v_buf[:, bi], v_new[:, ni])
    scores = jnp.einsum("nthgd,ntjhd->nhgtj", q, kg).astype(jnp.float32) * (dh ** -0.5)
    bias = -slopes[:, :, None, None] * (dilation * steps).astype(jnp.float32)
    scores = jnp.where(valid, scores + bias, NEG_INF)
    lse = jax.nn.logsumexp(scores, axis=-1)
    if sinks is not None:
        lse = jnp.logaddexp(lse, sinks.astype(jnp.float32)[:, :, None])
    p = jnp.exp(scores - lse[..., None]).astype(v_new.dtype)
    out = jnp.einsum("nhgtj,ntjhd->nthgd", p, vg)
    return out, jnp.moveaxis(lse, -1, 1)


def _diff_attend(q, q_pos, segments, slopes, lam, subln_g, lambda_init):
    dh = q.shape[-1]
    scores = jnp.concatenate([jnp.einsum("nthgmd,nkhmd->nhgmtk", q, k) for k, _, _ in segments],
                             axis=-1).astype(jnp.float32) * (dh ** -0.5)
    k_pos = jnp.concatenate([p for _, _, p in segments])
    dist = q_pos[:, None] - k_pos[None, :]
    bias = -slopes[:, :, None, None, None] * dist.astype(jnp.float32)
    scores = jnp.where(dist >= 0, scores + bias, NEG_INF)
    p = jax.nn.softmax(scores, axis=-1)
    attn = (p[:, :, :, 0] - lam * p[:, :, :, 1]).astype(q.dtype)
    out, start = None, 0
    for _, v, kp in segments:
        size = kp.shape[0]
        term = jnp.einsum("nhgtk,nkhe->nthge", attn[..., start:start + size], v)
        out = term if out is None else out + term
        start += size
    return _rms_norm(out, subln_g) * (1.0 - lambda_init)


def _sb_attend(q, q_pos, segments):
    dh = q.shape[-1]
    z = jnp.concatenate([jnp.einsum("nthgd,nkhd->nhgtk", q, k) for k, _, _ in segments],
                        axis=-1).astype(jnp.float32) * (dh ** -0.5)
    k_pos = jnp.concatenate([p for _, _, p in segments])
    earlier = k_pos[None, :] < q_pos[:, None]
    sp = jnp.where(earlier, jax.nn.softplus(z), 0.0)
    tail = lax.cumsum(sp, axis=z.ndim - 1, reverse=True)
    a = jnp.exp(jnp.where(earlier, z - tail, NEG_INF)).astype(q.dtype)
    out, start = None, 0
    for _, v, kp in segments:
        size = kp.shape[0]
        term = jnp.einsum("nhgtk,nkhd->nthgd", a[..., start:start + size], v)
        out = term if out is None else out + term
        start += size
    return out


def _conv_ffn(h, prev, w_a, conv_w, conv_b, w_g, w_down):
    t = h.shape[1]
    a = h @ w_a
    a_ext = jnp.concatenate([prev.astype(a.dtype), a], axis=1)
    c = sum(a_ext[:, j:j + t] * conv_w[j] for j in range(CONV_W)) + conv_b
    y = (jax.nn.silu(c) * (h @ w_g)) @ w_down
    return y, a_ext[:, t:]


def _even_project(h, w_in):
    n, t, _ = h.shape
    qa, ka, va, qb, kb, vb = _split_cols(h @ w_in, EVEN_SPLITS)
    return (qa.reshape(n, t, N_PATTERNS, A_HEADS, 1, HEAD_DIM),
            ka.reshape(n, t, N_PATTERNS, A_HEADS, HEAD_DIM),
            va.reshape(n, t, N_PATTERNS, A_HEADS, HEAD_DIM),
            qb.reshape(n, t, B_KV_HEADS, B_GROUP, HEAD_DIM),
            kb.reshape(n, t, B_KV_HEADS, HEAD_DIM),
            vb.reshape(n, t, B_KV_HEADS, HEAD_DIM))


def _even_merge(outs_a, lses_a, out_b, w_out):
    w = jax.nn.softmax(jnp.stack(lses_a), axis=0).astype(out_b.dtype)
    out_a = jnp.einsum("pnthg,pnthgd->nthgd", w, jnp.stack(outs_a))
    n, t = out_b.shape[:2]
    return jnp.concatenate([out_a.reshape(n, t, -1), out_b.reshape(n, t, -1)], axis=-1) @ w_out


def _even_mix_prompt(h, w_in, w_out, sinks):
    s = h.shape[1]
    qa, ka, va, qb, kb, vb = _even_project(h, w_in)
    a_slopes = _alibi_slopes(A_HEADS)[:, None]
    b_slopes = _alibi_slopes(B_HEADS).reshape(B_KV_HEADS, B_GROUP)
    outs, lses, rows = [], [], []
    for g, (win, dil) in enumerate(A_PATTERNS):
        o, l = _band_prompt(qa[:, :, g], ka[:, :, g], va[:, :, g], win, dil, a_slopes)
        outs.append(o)
        lses.append(l)
        rows.append(jnp.stack([ka[:, :, g], va[:, :, g]], axis=2)[:, s - min(win, s):])
    ob, _ = _band_prompt(qb, kb, vb, B_WINDOW, 1, b_slopes, sinks)
    rows.append(jnp.stack([kb, vb], axis=2)[:, s - min(B_WINDOW, s):])
    return _even_merge(outs, lses, ob, w_out), rows


def _even_mix_sample(h, bufs, i, w_in, w_out, sinks):
    qa, ka, va, qb, kb, vb = _even_project(h, w_in)
    a_slopes = _alibi_slopes(A_HEADS)[:, None]
    b_slopes = _alibi_slopes(B_HEADS).reshape(B_KV_HEADS, B_GROUP)
    outs, lses, rows = [], [], []
    for g, (win, dil) in enumerate(A_PATTERNS):
        o, l = _band_sample(qa[:, :, g], ka[:, :, g], va[:, :, g], bufs[g][i, :, :, 0], bufs[g][i, :, :, 1],
                            win, dil, a_slopes)
        outs.append(o)
        lses.append(l)
        rows.append(jnp.stack([ka[:, :, g], va[:, :, g]], axis=2))
    ob, _ = _band_sample(qb, kb, vb, bufs[N_PATTERNS][i, :, :, 0], bufs[N_PATTERNS][i, :, :, 1],
                         B_WINDOW, 1, b_slopes, sinks)
    rows.append(jnp.stack([kb, vb], axis=2))
    return _even_merge(outs, lses, ob, w_out), rows


def _odd_project(h, w_in):
    n, t, _ = h.shape
    qc, kc, vc, qd, kd, vd = _split_cols(h @ w_in, ODD_SPLITS)
    return (qc.reshape(n, t, C_KV_HEADS, C_GROUP, 2, C_HEAD_DIM),
            kc.reshape(n, t, C_KV_HEADS, 2 * C_HEAD_DIM),
            vc.reshape(n, t, C_KV_HEADS, 2 * C_HEAD_DIM),
            qd.reshape(n, t, D_KV_HEADS, D_GROUP, D_HEAD_DIM),
            kd.reshape(n, t, D_KV_HEADS, D_HEAD_DIM),
            vd.reshape(n, t, D_KV_HEADS, D_HEAD_DIM))


def _diff_keys(k):
    return k.reshape(k.shape[:3] + (2, C_HEAD_DIM))


def _odd_merge(oc, od, w_out):
    n, t = oc.shape[:2]
    return jnp.concatenate([oc.reshape(n, t, -1), od.reshape(n, t, -1)], axis=-1) @ w_out


def _odd_mix_prompt(h, w_in, w_out, lam, subln_g, lambda_init):
    n, s, _ = h.shape
    qc, kc, vc, qd, kd, vd = _odd_project(h, w_in)
    nb = s // BLOCK
    pos = jnp.arange(s)
    pos_b = pos.reshape(nb, BLOCK)

    def to_blocks(x):
        return jnp.moveaxis(x.reshape((n, nb, BLOCK) + x.shape[2:]), 1, 0)

    def from_blocks(x):
        return jnp.moveaxis(x, 0, 1).reshape((n, s) + x.shape[3:])

    c_slopes = _alibi_slopes(C_HEADS).reshape(C_KV_HEADS, C_GROUP)
    c_seg = ((_diff_keys(kc), vc, pos),)
    d_seg = ((kd, vd, pos),)
    oc = lax.map(lambda xs: _diff_attend(xs[0], xs[1], c_seg, c_slopes, lam, subln_g, lambda_init),
                 (to_blocks(qc), pos_b))
    od = lax.map(lambda xs: _sb_attend(xs[0], xs[1], d_seg), (to_blocks(qd), pos_b))
    return _odd_merge(from_blocks(oc), from_blocks(od), w_out), (kc, vc, kd, vd)


def _paged_rows(pool, i, page_table):
    rows = pool[i, page_table]
    return rows.reshape((rows.shape[0], rows.shape[1] * rows.shape[2]) + rows.shape[3:])


def _odd_mix_sample(h, pools, i, page_table, w_in, w_out, lam, subln_g, lambda_init):
    t = h.shape[1]
    qc, kc, vc, qd, kd, vd = _odd_project(h, w_in)
    past = page_table.shape[1] * PAGE_SIZE
    past_pos = jnp.arange(past)
    new_pos = past + jnp.arange(t)
    c_slopes = _alibi_slopes(C_HEADS).reshape(C_KV_HEADS, C_GROUP)
    oc = _diff_attend(qc, new_pos,
                      ((_diff_keys(_paged_rows(pools[0], i, page_table)), _paged_rows(pools[1], i, page_table), past_pos),
                       (_diff_keys(kc), vc, new_pos)),
                      c_slopes, lam, subln_g, lambda_init)
    od = _sb_attend(qd, new_pos,
                    ((_paged_rows(pools[2], i, page_table), _paged_rows(pools[3], i, page_table), past_pos),
                     (kd, vd, new_pos)))
    return _odd_merge(oc, od, w_out), (kc, vc, kd, vd)


def _stack_layers(per_layer, j):
    return jnp.stack([entry[j] for entry in per_layer])


def setup_inputs(seed: int = 0) -> dict:
    key = jax.random.key(seed)
    keys = iter(jax.random.split(key, 40))

    def nrm(shape, scale=1.0):
        return jax.random.normal(next(keys), shape, jnp.float32) * scale

    n_pages = PAST_LEN // PAGE_SIZE
    n_used = DEC_BATCH * n_pages
    n_phys = n_used + max(1, n_used // 4)

    def v_scale(splits):
        return jnp.asarray(np.concatenate(
            [np.full(sz, BETA if j in V_COLUMNS else 1.0, np.float32) for j, sz in enumerate(splits)]))

    inp = {}
    inp["x_prompt"] = nrm((BATCH, SEQ, D_MODEL))
    inp["x_sample"] = nrm((DEC_BATCH, DEC_SEQ, D_MODEL))
    for name, (win, _) in zip(("cache_a1", "cache_a2", "cache_a3"), A_PATTERNS):
        inp[name] = nrm((N_EVEN, DEC_BATCH, min(win, PAST_LEN), 2, A_HEADS, HEAD_DIM))
    inp["cache_b"] = nrm((N_EVEN, DEC_BATCH, min(B_WINDOW, PAST_LEN), 2, B_KV_HEADS, HEAD_DIM))
    inp["cache_c_k"] = nrm((N_ODD, n_phys, PAGE_SIZE, C_KV_HEADS, 2 * C_HEAD_DIM))
    inp["cache_c_v"] = nrm((N_ODD, n_phys, PAGE_SIZE, C_KV_HEADS, 2 * C_HEAD_DIM))
    inp["cache_d_k"] = nrm((N_ODD, n_phys, PAGE_SIZE, D_KV_HEADS, D_HEAD_DIM))
    inp["cache_d_v"] = nrm((N_ODD, n_phys, PAGE_SIZE, D_KV_HEADS, D_HEAD_DIM))
    inp["state_conv"] = nrm((DEPTH, DEC_BATCH, CONV_W - 1, D_FF), BETA)
    perm = jax.random.permutation(next(keys), n_phys)
    inp["page_table"] = perm[:n_used].reshape(DEC_BATCH, n_pages).astype(jnp.int32)
    inp["w_in_even"] = nrm((N_EVEN, D_MODEL, EVEN_IN), D_MODEL ** -0.5) * v_scale(EVEN_SPLITS)
    inp["w_out_even"] = nrm((N_EVEN, EVEN_OUT, D_MODEL), BETA * EVEN_OUT ** -0.5)
    inp["sinks_b"] = nrm((N_EVEN, B_KV_HEADS, B_GROUP))
    inp["w_in_odd"] = nrm((N_ODD, D_MODEL, ODD_IN), D_MODEL ** -0.5) * v_scale(ODD_SPLITS)
    inp["w_out_odd"] = nrm((N_ODD, ODD_OUT, D_MODEL), BETA * ODD_OUT ** -0.5)
    inp["lam_q1"] = nrm((N_ODD, C_HEAD_DIM), 0.1)
    inp["lam_k1"] = nrm((N_ODD, C_HEAD_DIM), 0.1)
    inp["lam_q2"] = nrm((N_ODD, C_HEAD_DIM), 0.1)
    inp["lam_k2"] = nrm((N_ODD, C_HEAD_DIM), 0.1)
    inp["subln_g"] = 1.0 + nrm((N_ODD, 2 * C_HEAD_DIM), 0.02)
    inp["w_ffn_a"] = nrm((DEPTH, D_MODEL, D_FF), BETA * D_MODEL ** -0.5)
    inp["conv_w"] = nrm((DEPTH, CONV_W, D_FF), CONV_W ** -0.5)
    inp["conv_b"] = nrm((DEPTH, D_FF), 0.02)
    inp["w_ffn_g"] = nrm((DEPTH, D_MODEL, D_FF), BETA * D_MODEL ** -0.5)
    inp["w_ffn_down"] = nrm((DEPTH, D_FF, D_MODEL), BETA * D_FF ** -0.5)
    inp["ln_mix_g"] = 1.0 + nrm((DEPTH, D_MODEL), 0.02)
    inp["ln_mix_b"] = nrm((DEPTH, D_MODEL), 0.02)
    inp["ln_ffn_g"] = 1.0 + nrm((DEPTH, D_MODEL), 0.02)
    inp["ln_ffn_b"] = nrm((DEPTH, D_MODEL), 0.02)
    return inp


def reference(x_prompt, x_sample, cache_a1, cache_a2, cache_a3, cache_b, cache_c_k, cache_c_v, cache_d_k, cache_d_v,
              state_conv, page_table, w_in_even, w_out_even, sinks_b, w_in_odd, w_out_odd,
              lam_q1, lam_k1, lam_q2, lam_k2, subln_g, w_ffn_a, conv_w, conv_b, w_ffn_g, w_ffn_down,
              ln_mix_g, ln_mix_b, ln_ffn_g, ln_ffn_b):
    win_bufs = (cache_a1, cache_a2, cache_a3, cache_b)
    pools = (cache_c_k, cache_c_v, cache_d_k, cache_d_v)
    xp, xs = x_prompt, x_sample
    even_p, even_s, odd_p, odd_s, conv_p, conv_s = [], [], [], [], [], []
    for layer in range(DEPTH):
        i = layer // 2
        if layer % 2 == 0:
            mp, rows_p = _even_mix_prompt(xp, w_in_even[i], w_out_even[i], sinks_b[i])
            ms, rows_s = _even_mix_sample(xs, win_bufs, i, w_in_even[i], w_out_even[i], sinks_b[i])
            even_p.append(rows_p)
            even_s.append(rows_s)
        else:
            lambda_init = 0.8 - 0.6 * math.exp(-0.3 * layer)
            lam = (jnp.exp(jnp.sum((lam_q1[i] * lam_k1[i]).astype(jnp.float32)))
                   - jnp.exp(jnp.sum((lam_q2[i] * lam_k2[i]).astype(jnp.float32))) + lambda_init)
            mp, rows_p = _odd_mix_prompt(xp, w_in_odd[i], w_out_odd[i], lam, subln_g[i], lambda_init)
            ms, rows_s = _odd_mix_sample(xs, pools, i, page_table, w_in_odd[i], w_out_odd[i], lam, subln_g[i],
                                         lambda_init)
            odd_p.append(rows_p)
            odd_s.append(rows_s)
        xp = _layer_norm(ALPHA * xp + mp, ln_mix_g[layer], ln_mix_b[layer])
        xs = _layer_norm(ALPHA * xs + ms, ln_mix_g[layer], ln_mix_b[layer])
        fp, cp = _conv_ffn(xp, jnp.zeros((xp.shape[0], CONV_W - 1, D_FF), xp.dtype), w_ffn_a[layer],
                           conv_w[layer], conv_b[layer], w_ffn_g[layer], w_ffn_down[layer])
        fs, cs = _conv_ffn(xs, state_conv[layer], w_ffn_a[layer], conv_w[layer], conv_b[layer],
                           w_ffn_g[layer], w_ffn_down[layer])
        xp = _layer_norm(ALPHA * xp + fp, ln_ffn_g[layer], ln_ffn_b[layer])
        xs = _layer_norm(ALPHA * xs + fs, ln_ffn_g[layer], ln_ffn_b[layer])
        conv_p.append(cp)
        conv_s.append(cs)
    new_a1_prompt, new_a1_sample = _stack_layers(even_p, 0), _stack_layers(even_s, 0)
    new_a2_prompt, new_a2_sample = _stack_layers(even_p, 1), _stack_layers(even_s, 1)
    new_a3_prompt, new_a3_sample = _stack_layers(even_p, 2), _stack_layers(even_s, 2)
    new_b_prompt, new_b_sample = _stack_layers(even_p, 3), _stack_layers(even_s, 3)
    new_c_k_prompt, new_c_k_sample = _stack_layers(odd_p, 0), _stack_layers(odd_s, 0)
    new_c_v_prompt, new_c_v_sample = _stack_layers(odd_p, 1), _stack_layers(odd_s, 1)
    new_d_k_prompt, new_d_k_sample = _stack_layers(odd_p, 2), _stack_layers(odd_s, 2)
    new_d_v_prompt, new_d_v_sample = _stack_layers(odd_p, 3), _stack_layers(odd_s, 3)
    new_conv_prompt, new_conv_sample = jnp.stack(conv_p), jnp.stack(conv_s)
    return (xp, xs, new_a1_prompt, new_a1_sample, new_a2_prompt, new_a2_sample, new_a3_prompt, new_a3_sample,
            new_b_prompt, new_b_sample, new_c_k_prompt, new_c_k_sample, new_c_v_prompt, new_c_v_sample,
            new_d_k_prompt, new_d_k_sample, new_d_v_prompt, new_d_v_sample, new_conv_prompt, new_conv_sample)
```

```python
import functools
import math

import numpy as np
import jax
import jax.numpy as jnp
from jax import lax
from jax.experimental import pallas as pl
from jax.experimental.pallas import tpu as pltpu

F32 = jnp.float32
BF16 = jnp.bfloat16

D_MODEL = 2048
DEPTH = 2
PAGE = 128
HD = 64
BLK = 128
A_PATTERNS = ((128, 1), (512, 4), (2048, 16))
A_HEADS = 8
B_HEADS = 8
B_KV = 2
C_HEADS = 8
C_KV = 2
D_HEADS = 16
D_KV = 4
D_FF = 5632
LN_EPS = 1e-5
NEG = -1e30
ALPHA = (2 * DEPTH) ** 0.25
EVEN_IN = 5376
EVEN_PAD = 5632
ODD_IN = 3072
SCALE = HD ** -0.5

VMEM_LIMIT = 56 * 1024 * 1024


def _params(sem):
    return pltpu.CompilerParams(dimension_semantics=sem, vmem_limit_bytes=VMEM_LIMIT)


def _nt(a, b):
    return lax.dot_general(a, b, (((1,), (1,)), ((), ())), preferred_element_type=F32)


def _mm(a, b):
    return jnp.dot(a, b, preferred_element_type=F32)


def _slope(h, n=8):
    return 2.0 ** (-8.0 * (h + 1) / n)


def _matmul_kernel(x_ref, w_ref, o_ref, xb_ref):
    @pl.when(pl.program_id(1) == 0)
    def _():
        xb_ref[...] = x_ref[...].astype(BF16)

    o_ref[...] = _mm(xb_ref[...], w_ref[...])


def _matmul(x, w, tm, tn):
    m, k = x.shape
    n = w.shape[1]
    tm = min(tm, m)
    return pl.pallas_call(
        _matmul_kernel,
        grid=(m // tm, n // tn),
        in_specs=[pl.BlockSpec((tm, k), lambda i, j: (i, 0)),
                  pl.BlockSpec((k, tn), lambda i, j: (0, j))],
        out_specs=pl.BlockSpec((tm, tn), lambda i, j: (i, j)),
        out_shape=jax.ShapeDtypeStruct((m, n), F32),
        scratch_shapes=[pltpu.VMEM((tm, k), BF16)],
        compiler_params=_params(("parallel", "arbitrary")),
        name="proj_matmul",
    )(x, w)


def _layer_norm_rows(r, g, b):
    mu = jnp.mean(r, axis=-1, keepdims=True)
    d = r - mu
    var = jnp.mean(d * d, axis=-1, keepdims=True)
    return d * lax.rsqrt(var + LN_EPS) * g + b


def _outproj_ln_kernel(m1_ref, m2_ref, w1_ref, w2_ref, x_ref, g_ref, b_ref, o_ref):
    y = _mm(m1_ref[...].astype(BF16), w1_ref[...]) + _mm(m2_ref[...].astype(BF16), w2_ref[...])
    o_ref[...] = _layer_norm_rows(ALPHA * x_ref[...] + y, g_ref[...], b_ref[...])


def _outproj_ln(m1, m2, w1, w2, x, g, b, tm=256):
    m, d = x.shape
    k1, k2 = m1.shape[1], m2.shape[1]
    tm = min(tm, m)
    row = lambda i: (i, 0)
    fixed = lambda i: (0, 0)
    return pl.pallas_call(
        _outproj_ln_kernel,
        grid=(m // tm,),
        in_specs=[pl.BlockSpec((tm, k1), row), pl.BlockSpec((tm, k2), row),
                  pl.BlockSpec((k1, d), fixed), pl.BlockSpec((k2, d), fixed),
                  pl.BlockSpec((tm, d), row), pl.BlockSpec((1, d), fixed), pl.BlockSpec((1, d), fixed)],
        out_specs=pl.BlockSpec((tm, d), row),
        out_shape=jax.ShapeDtypeStruct((m, d), F32),
        compiler_params=_params(("parallel",)),
        name="outproj_ln",
    )(m1, m2, w1, w2, x, g.reshape(1, d), b.reshape(1, d))


FFN_HALO = 16


def _ffn_kernel(*refs, tm, sample, t_len):
    if sample:
        x_ref, p1_ref, p2_ref, wa_ref, wg_ref, wd_ref, cw_ref, cb_ref, g_ref, b_ref, o_ref, tail_ref, xb_ref, acc_ref = refs
    else:
        x_ref, xh_ref, wa_ref, wg_ref, wd_ref, cw_ref, cb_ref, g_ref, b_ref, o_ref, tail_ref, xb_ref, acc_ref = refs
    i = pl.program_id(0)
    f = pl.program_id(1)

    @pl.when(f == 0)
    def _():
        xb_ref[FFN_HALO:, :] = x_ref[...].astype(BF16)
        if sample:
            xb_ref[:FFN_HALO, :] = jnp.zeros((FFN_HALO, x_ref.shape[1]), BF16)
        else:
            halo = jnp.where(i > 0, xh_ref[...], 0.0)
            xb_ref[:FFN_HALO, :] = halo.astype(BF16)
        acc_ref[...] = jnp.zeros_like(acc_ref)

    a_ext = _mm(xb_ref[...], wa_ref[...])
    gate = _mm(xb_ref[FFN_HALO:, :], wg_ref[...])
    a = a_ext[FFN_HALO:, :]
    a1 = pltpu.roll(a_ext, 1, 0)[FFN_HALO:, :]
    a2 = pltpu.roll(a_ext, 2, 0)[FFN_HALO:, :]
    if sample:
        t = lax.broadcasted_iota(jnp.int32, a.shape, 0) % t_len
        a1 = jnp.where(t == 0, p1_ref[...], a1)
        a2 = jnp.where(t < 2, p2_ref[...], a2)
    cw = cw_ref[...]
    c = a2 * cw[0:1, :] + a1 * cw[1:2, :] + a * cw[2:3, :] + cb_ref[...]
    h = (c * jax.nn.sigmoid(c)) * gate
    acc_ref[...] += _mm(h.astype(BF16), wd_ref[...])
    tail_ref[...] = a[tm - tail_ref.shape[0]:, :]

    @pl.when(f == pl.num_programs(1) - 1)
    def _():
        o_ref[...] = _layer_norm_rows(ALPHA * x_ref[...] + acc_ref[...], g_ref[...], b_ref[...])


def _ffn(x, wa, wg, wd, cw, cb, g, b, prev=None, t_len=4, tm=512, tf=512):
    m, d = x.shape
    dff = wa.shape[1]
    tm = min(tm, m)
    sample = prev is not None
    row = lambda i, f: (i, 0)
    col = lambda i, f: (0, f)
    wcol = pl.BlockSpec((d, tf), col)
    ins = [x]
    specs = [pl.BlockSpec((tm, d), row)]
    if sample:
        ins += list(prev)
        specs += [pl.BlockSpec((tm, tf), lambda i, f: (i, f))] * 2
        tail_rows = tm
        tail_spec = pl.BlockSpec((tm, tf), lambda i, f: (i, f))
        tail_shape = (m, dff)
    else:
        ins += [x]
        hb = tm // FFN_HALO
        specs += [pl.BlockSpec((FFN_HALO, d), lambda i, f: (jnp.maximum(i * hb - 1, 0), 0))]
        tail_rows = 8
        tail_spec = pl.BlockSpec((8, tf), lambda i, f: (i, f))
        tail_shape = (8 * (m // tm), dff)
    ins += [wa, wg, wd, cw, cb.reshape(1, dff), g.reshape(1, d), b.reshape(1, d)]
    specs += [wcol, wcol, pl.BlockSpec((tf, d), lambda i, f: (f, 0)),
              pl.BlockSpec((3, tf), col), pl.BlockSpec((1, tf), col),
              pl.BlockSpec((1, d), lambda i, f: (0, 0)), pl.BlockSpec((1, d), lambda i, f: (0, 0))]
    return pl.pallas_call(
        functools.partial(_ffn_kernel, tm=tm, sample=sample, t_len=t_len),
        grid=(m // tm, dff // tf),
        in_specs=specs,
        out_specs=[pl.BlockSpec((tm, d), row), tail_spec],
        out_shape=[jax.ShapeDtypeStruct((m, d), F32), jax.ShapeDtypeStruct(tail_shape, F32)],
        scratch_shapes=[pltpu.VMEM((tm + FFN_HALO, d), BF16), pltpu.VMEM((tm, d), F32)],
        compiler_params=_params(("arbitrary", "arbitrary")),
        name="conv_ffn_sample" if sample else "conv_ffn_prompt",
    )(*ins)


def _band_prompt_kernel(*refs, dil, kv_off, slopes, has_sink, has_prev, want_lse):
    it = iter(refs)
    q_ref, kc_ref, kp_ref, vc_ref, vp_ref = (next(it) for _ in range(5))
    sink_ref = next(it) if has_sink else None
    op_ref, lp_ref = (next(it), next(it)) if has_prev else (None, None)
    o_ref = next(it)
    l_ref = next(it) if want_lse else None
    b = pl.program_id(1)
    qi = lax.broadcasted_iota(jnp.int32, (BLK, BLK), 0)
    kj = lax.broadcasted_iota(jnp.int32, (BLK, BLK), 1)
    mask_c = kj <= qi
    mask_p = jnp.logical_and(kj >= qi, b > 0)
    dist_c = (qi - kj).astype(F32) * float(dil)
    dist_p = (BLK + qi - kj).astype(F32) * float(dil)
    for h in range(len(kv_off)):
        hs = slice(h * HD, (h + 1) * HD)
        ks = slice(kv_off[h], kv_off[h] + HD)
        q = q_ref[:, hs].astype(BF16)
        s_c = _nt(q, kc_ref[:, ks].astype(BF16)) * SCALE - slopes[h] * dist_c
        s_p = _nt(q, kp_ref[:, ks].astype(BF16)) * SCALE - slopes[h] * dist_p
        s_c = jnp.where(mask_c, s_c, NEG)
        s_p = jnp.where(mask_p, s_p, NEG)
        m = jnp.maximum(jnp.max(s_c, axis=-1, keepdims=True), jnp.max(s_p, axis=-1, keepdims=True))
        p_c = jnp.exp(s_c - m)
        p_p = jnp.exp(s_p - m)
        l = jnp.sum(p_c, axis=-1, keepdims=True) + jnp.sum(p_p, axis=-1, keepdims=True)
        if has_sink:
            l = l + jnp.exp(sink_ref[h] - m)
        o = _mm(p_c.astype(BF16), vc_ref[:, ks].astype(BF16)) + _mm(p_p.astype(BF16), vp_ref[:, ks].astype(BF16))
        o = o / l
        lse = m + jnp.log(l)
        if has_prev:
            lse_prev = lp_ref[:, hs]
            lse_new = jnp.logaddexp(lse_prev, lse)
            o = op_ref[:, hs] * jnp.exp(lse_prev - lse_new) + o * jnp.exp(lse - lse_new)
            lse = lse_new
        o_ref[:, hs] = o
        if want_lse:
            l_ref[:, hs] = jnp.broadcast_to(lse, (BLK, HD))


def _band_prompt(proj, dil, q_col, k_col, v_col, kv_w, kv_off, slopes, sinks=None, prev=None, want_lse=False):
    s, w = proj.shape
    length = s // dil
    nb = length // BLK
    view = proj.reshape(length, dil * w)
    nh = len(kv_off)
    qw = nh * HD
    qspec = pl.BlockSpec((BLK, qw), lambda r, b: (b, r * (w // qw) + q_col // qw))
    kc = pl.BlockSpec((BLK, kv_w), lambda r, b: (b, r * (w // kv_w) + k_col // kv_w))
    kp = pl.BlockSpec((BLK, kv_w), lambda r, b: (jnp.maximum(b - 1, 0), r * (w // kv_w) + k_col // kv_w))
    vc = pl.BlockSpec((BLK, kv_w), lambda r, b: (b, r * (w // kv_w) + v_col // kv_w))
    vp = pl.BlockSpec((BLK, kv_w), lambda r, b: (jnp.maximum(b - 1, 0), r * (w // kv_w) + v_col // kv_w))
    ospec = pl.BlockSpec((BLK, qw), lambda r, b: (b, r))
    ins = [view, view, view, view, view]
    specs = [qspec, kc, kp, vc, vp]
    if sinks is not None:
        ins.append(sinks)
        specs.append(pl.BlockSpec(memory_space=pltpu.SMEM))
    if prev is not None:
        ins += [prev[0].reshape(length, dil * qw), prev[1].reshape(length, dil * qw)]
        specs += [ospec, ospec]
    oshape = jax.ShapeDtypeStruct((length, dil * qw), F32)
    outs = pl.pallas_call(
        functools.partial(_band_prompt_kernel, dil=dil, kv_off=tuple(kv_off), slopes=tuple(slopes),
                          has_sink=sinks is not None, has_prev=prev is not None, want_lse=want_lse),
        grid=(dil, nb),
        in_specs=specs,
        out_specs=[ospec, ospec] if want_lse else ospec,
        out_shape=[oshape, oshape] if want_lse else oshape,
        compiler_params=_params(("parallel", "parallel")),
        name=f"band_prompt_d{dil}",
    )(*ins)
    if want_lse:
        return outs[0].reshape(s, qw), outs[1].reshape(s, qw)
    return outs.reshape(s, qw)


def _col_consts(values, rows):
    r = lax.broadcasted_iota(jnp.int32, (rows, 1), 0)
    col = jnp.zeros((rows, 1), F32)
    for h, v in enumerate(values):
        col = jnp.where(r == h, v, col)
    return col


def _sample_even_kernel(ps_ref, a1_ref, a2_ref, a3_ref, cb_ref, fold_ref, sink_ref, oa_ref, ob_ref, *, t_len):
    nh = 8
    nr = t_len * nh
    row = lax.broadcasted_iota(jnp.int32, (nr, 1), 0)
    t_col = row // nh
    h_col = row % nh
    hm = (lax.broadcasted_iota(jnp.int32, (nr, nh * HD), 1) // HD == h_col).astype(F32)
    slope = jnp.zeros((nr, 1), F32)
    sink_col = jnp.zeros((nr, 1), F32)
    for h in range(nh):
        slope = jnp.where(h_col == h, _slope(h), slope)
        sink_col = jnp.where(h_col == h, sink_ref[h], sink_col)

    def q_rows(col):
        return jnp.concatenate([jnp.broadcast_to(ps_ref[t:t + 1, col:col + 512], (nh, 512))
                                for t in range(t_len)], axis=0) * hm

    def attend(qk, q_new, kt_ref_slice, vt_ref_slice, k_col, v_col, width, dil, sinks):
        kt = kt_ref_slice.astype(BF16)
        length = kt.shape[1]
        dd = length + t_col - lax.broadcasted_iota(jnp.int32, (nr, length), 1)
        valid = jnp.logical_and((dd & (dil - 1)) == 0, dd <= dil * BLK)
        s = jnp.where(valid, _mm(qk, kt) * SCALE - slope * dd.astype(F32), NEG)
        m = jnp.max(s, axis=-1, keepdims=True)
        new = []
        for tp in range(t_len):
            dn = t_col - tp
            ok = jnp.logical_and(dn >= 0, (dn & (dil - 1)) == 0)
            sc = jnp.sum(q_new * ps_ref[tp:tp + 1, k_col:k_col + width], axis=-1, keepdims=True) * SCALE
            sc = jnp.where(ok, sc - slope * dn.astype(F32), NEG)
            new.append(sc)
            m = jnp.maximum(m, sc)
        p = jnp.exp(s - m)
        l = jnp.sum(p, axis=-1, keepdims=True)
        o = _nt(p.astype(BF16), vt_ref_slice.astype(BF16))
        for tp in range(t_len):
            pn = jnp.exp(new[tp] - m)
            l = l + pn
            o = o + pn * ps_ref[tp:tp + 1, v_col:v_col + width]
        if sinks:
            l = l + jnp.exp(sink_col - m)
        return o / l, m + jnp.log(l)

    def head_rows(o_wide):
        return jnp.sum((o_wide * hm).reshape(t_len, nh, nh * HD), axis=1)

    outs, lses = [], []
    for g, (buf, (_, dil)) in enumerate(zip((a1_ref, a2_ref, a3_ref), A_PATTERNS)):
        q = q_rows(g * 512)
        o, lse = attend(q.astype(BF16), q, buf[0:512, :], buf[512:1024, :],
                        1536 + g * 512, 3072 + g * 512, 512, dil, False)
        outs.append(o)
        lses.append(lse)
    mx = jnp.maximum(jnp.maximum(lses[0], lses[1]), lses[2])
    es = [jnp.exp(l_ - mx) for l_ in lses]
    merged = (es[0] * outs[0] + es[1] * outs[1] + es[2] * outs[2]) / (es[0] + es[1] + es[2])
    oa_ref[...] = head_rows(merged)

    qf = _mm(q_rows(4608).astype(BF16), fold_ref[...])
    o, _ = attend(qf.astype(BF16), qf, cb_ref[0:128, :], cb_ref[128:256, :], 5120, 5248, 128, 1, True)
    o_sw = pltpu.roll(o, HD, 1)
    o_sel = jnp.where((h_col // 4) == (h_col % 2), o, o_sw)
    ob_ref[...] = head_rows(jnp.concatenate([o_sel] * 4, axis=1))


def _fold_matrix():
    c = np.arange(512)[:, None]
    l = np.arange(128)[None, :]
    return jnp.asarray(((c % HD == l % HD) & (l // HD == (c // HD) // 4)).astype(np.float32), BF16)


def _sample_even(ps, a1, a2, a3, cb, layer, sinks, t_len=4):
    nl, n = a1.shape[0], a1.shape[1]
    ps3 = ps.reshape(n, t_len, ps.shape[1])

    def rows_last(c):
        return jnp.transpose(c, (0, 1, 3, 4, 5, 2)).reshape(nl * n, -1, c.shape[2])

    a1v, a2v, a3v, cbv = rows_last(a1), rows_last(a2), rows_last(a3), rows_last(cb)
    b3 = lambda arr: pl.BlockSpec((None,) + arr.shape[1:], lambda i: (layer * n + i, 0, 0))
    ospec = pl.BlockSpec((None, t_len, 512), lambda i: (i, 0, 0))
    oa, ob = pl.pallas_call(
        functools.partial(_sample_even_kernel, t_len=t_len),
        grid=(n,),
        in_specs=[pl.BlockSpec((None, t_len, ps.shape[1]), lambda i: (i, 0, 0)),
                  b3(a1v), b3(a2v), b3(a3v), b3(cbv),
                  pl.BlockSpec((512, 128), lambda i: (0, 0)),
                  pl.BlockSpec(memory_space=pltpu.SMEM)],
        out_specs=[ospec, ospec],
        out_shape=[jax.ShapeDtypeStruct((n, t_len, 512), F32)] * 2,
        compiler_params=_params(("parallel",)),
        name="sample_even_attn",
    )(ps3, a1v, a2v, a3v, cbv, _fold_matrix(), sinks)
    return oa.reshape(n * t_len, 512), ob.reshape(n * t_len, 512)


C_GROUP = C_HEADS // C_KV
D_GROUP = D_HEADS // D_KV


def _lambda(lamv, lambda_init):
    s1 = jnp.sum(lamv[0:1, :] * lamv[1:2, :], axis=-1, keepdims=True)
    s2 = jnp.sum(lamv[2:3, :] * lamv[3:4, :], axis=-1, keepdims=True)
    return jnp.exp(s1) - jnp.exp(s2) + lambda_init


def _sub_rms(o0, o1, lam, g, lambda_init):
    d = o0 - lam * o1
    return d * lax.rsqrt(jnp.mean(d * d, axis=-1, keepdims=True) + LN_EPS) * g * (1.0 - lambda_init)


def _softplus(z):
    return jnp.maximum(z, 0.0) + jnp.log1p(jnp.exp(-jnp.abs(z)))


def _tri_matrix(n):
    u = (np.arange(n)[:, None] >= np.arange(n)[None, :]).astype(np.float32)
    return jnp.asarray(np.concatenate([u, u], axis=0), BF16)


def _rev_cumsum(sp, uu):
    hi = sp.astype(BF16)
    lo = (sp - hi.astype(F32)).astype(BF16)
    return _mm(jnp.concatenate([hi, lo], axis=1), uu)


def _diff_prompt_kernel(q_ref, k_ref, v_ref, lamv_ref, g_ref, o_ref, m_ref, l_ref, acc_ref, *, tq, tk, lambda_init):
    hk = pl.program_id(0)
    i = pl.program_id(1)
    rows = C_GROUP * tq
    r = lax.broadcasted_iota(jnp.int32, (rows, 1), 0)
    base = jnp.where(hk == 0, 1.0, 2.0 ** (-C_GROUP)).astype(F32)
    slope = jnp.zeros((rows, 1), F32)
    for g in range(C_GROUP):
        slope = jnp.where(r // tq == g, _slope(g), slope)
    slope = slope * base
    qpos = i * tq + r % tq
    qs = []
    for m in range(2):
        parts = [q_ref[:, (g * 2 + m) * HD:(g * 2 + m + 1) * HD] for g in range(C_GROUP)]
        qs.append((jnp.concatenate(parts, axis=0) * SCALE).astype(BF16))
    m_ref[...] = jnp.full(m_ref.shape, NEG, F32)
    l_ref[...] = jnp.zeros_like(l_ref)
    acc_ref[...] = jnp.zeros_like(acc_ref)

    def step(j, carry):
        start = pl.multiple_of(j * tk, tk)
        k = k_ref[pl.ds(start, tk), :].astype(BF16)
        v = v_ref[pl.ds(start, tk), :].astype(BF16)
        kpos = start + lax.broadcasted_iota(jnp.int32, (1, tk), 1)
        dist = qpos - kpos
        bias = -slope * dist.astype(F32)
        valid = dist >= 0
        for m in range(2):
            s = jnp.where(valid, _nt(qs[m], k[:, m * HD:(m + 1) * HD]) + bias, NEG)
            m_old = m_ref[m]
            m_new = jnp.maximum(m_old, jnp.max(s, axis=-1, keepdims=True))
            alpha = jnp.exp(m_old - m_new)
            p = jnp.exp(s - m_new)
            l_ref[m] = alpha * l_ref[m] + jnp.sum(p, axis=-1, keepdims=True)
            acc_ref[m] = alpha * acc_ref[m] + _mm(p.astype(BF16), v)
            m_ref[m] = m_new
        return carry

    lax.fori_loop(0, (i * tq) // tk + 1, step, 0)
    lam = _lambda(lamv_ref[...], lambda_init)
    o = _sub_rms(acc_ref[0] / l_ref[0], acc_ref[1] / l_ref[1], lam, g_ref[...], lambda_init)
    for g in range(C_GROUP):
        o_ref[:, g * 2 * HD:(g + 1) * 2 * HD] = o[g * tq:(g + 1) * tq, :]


def _diff_prompt(proj, lamv, subln_g, lambda_init, tq=256, tk=256):
    s = proj.shape[0]
    rows = C_GROUP * tq
    return pl.pallas_call(
        functools.partial(_diff_prompt_kernel, tq=tq, tk=tk, lambda_init=lambda_init),
        grid=(C_KV, s // tq),
        in_specs=[pl.BlockSpec((tq, 512), lambda h, i: (i, h)),
                  pl.BlockSpec((s, 128), lambda h, i: (0, 8 + h)),
                  pl.BlockSpec((s, 128), lambda h, i: (0, 10 + h)),
                  pl.BlockSpec((4, HD), lambda h, i: (0, 0)),
                  pl.BlockSpec((1, 2 * HD), lambda h, i: (0, 0))],
        out_specs=pl.BlockSpec((tq, 512), lambda h, i: (i, h)),
        out_shape=jax.ShapeDtypeStruct((s, C_HEADS * 2 * HD), F32),
        scratch_shapes=[pltpu.VMEM((2, rows, 1), F32), pltpu.VMEM((2, rows, 1), F32),
                        pltpu.VMEM((2, rows, 2 * HD), F32)],
        compiler_params=_params(("parallel", "arbitrary")),
        name="diff_attn_prompt",
    )(proj, proj, proj, lamv, subln_g.reshape(1, 2 * HD))


def _sb_prompt_kernel(q_ref, k_ref, v_ref, uu_ref, o_ref, c_ref, acc_ref, *, tq, tk):
    i = pl.program_id(1)
    rows = D_GROUP * tq
    r = lax.broadcasted_iota(jnp.int32, (rows, 1), 0)
    qpos = i * tq + r % tq
    qs = []
    for hl in range(2):
        parts = [q_ref[:, (hl * D_GROUP + g) * HD:(hl * D_GROUP + g + 1) * HD] for g in range(D_GROUP)]
        qs.append((jnp.concatenate(parts, axis=0) * SCALE).astype(BF16))
    c_ref[...] = jnp.zeros_like(c_ref)
    acc_ref[...] = jnp.zeros_like(acc_ref)
    nblk = (i * tq) // tk + 1

    def step(it, carry):
        start = pl.multiple_of((nblk - 1 - it) * tk, tk)
        kpos = start + lax.broadcasted_iota(jnp.int32, (1, tk), 1)
        earlier = kpos < qpos
        for hl in range(2):
            k = k_ref[pl.ds(start, tk), hl * HD:(hl + 1) * HD].astype(BF16)
            v = v_ref[pl.ds(start, tk), hl * HD:(hl + 1) * HD].astype(BF16)
            z = _nt(qs[hl], k)
            sp = jnp.where(earlier, _softplus(z), 0.0)
            tl = _rev_cumsum(sp, uu_ref[...])
            c = c_ref[hl]
            a = jnp.exp(jnp.where(earlier, z - tl - c, NEG))
            acc_ref[hl] += _mm(a.astype(BF16), v)
            c_ref[hl] = c + tl[:, 0:1]
        return carry

    lax.fori_loop(0, nblk, step, 0)
    for hl in range(2):
        for g in range(D_GROUP):
            o_ref[:, (hl * D_GROUP + g) * HD:(hl * D_GROUP + g + 1) * HD] = acc_ref[hl, g * tq:(g + 1) * tq, :]


def _sb_prompt(proj, tq=256, tk=256):
    s = proj.shape[0]
    rows = D_GROUP * tq
    return pl.pallas_call(
        functools.partial(_sb_prompt_kernel, tq=tq, tk=tk),
        grid=(D_KV // 2, s // tq),
        in_specs=[pl.BlockSpec((tq, 512), lambda h, i: (i, 3 + h)),
                  pl.BlockSpec((s, 128), lambda h, i: (0, 20 + h)),
                  pl.BlockSpec((s, 128), lambda h, i: (0, 22 + h)),
                  pl.BlockSpec((2 * tk, tk), lambda h, i: (0, 0))],
        out_specs=pl.BlockSpec((tq, 512), lambda h, i: (i, h)),
        out_shape=jax.ShapeDtypeStruct((s, D_HEADS * HD), F32),
        scratch_shapes=[pltpu.VMEM((2, rows, 1), F32), pltpu.VMEM((2, rows, HD), F32)],
        compiler_params=_params(("parallel", "arbitrary")),
        name="sb_attn_prompt",
    )(proj, proj, proj, _tri_matrix(tk))


PAGES_PER_STEP = 8
QROWS = 64


def _sample_odd_kernel(pt_ref, qc_ref, qd_ref, ps_ref, *refs, npages, t_len, past_len, lambda_init):
    pp = PAGES_PER_STEP
    hrows = QROWS // C_KV
    kc_refs, vc_refs = refs[0:pp], refs[pp:2 * pp]
    kd_refs, vd_refs = refs[2 * pp:3 * pp], refs[3 * pp:4 * pp]
    uu_ref, lamv_ref, g_ref, oc_ref, od_ref, m_ref, l_ref, accc_ref, c_ref, accd_ref = refs[4 * pp:]
    j = pl.program_id(1)
    r = lax.broadcasted_iota(jnp.int32, (QROWS, 1), 0)
    tc_col = (r // C_GROUP) % t_len
    td_col = (r // D_GROUP) % t_len
    head_c = (r // (2 * t_len * C_GROUP)) * C_GROUP + r % C_GROUP
    slope = jnp.zeros((QROWS, 1), F32)
    for h in range(C_HEADS):
        slope = jnp.where(head_c == h, _slope(h), slope)
    qc = qc_ref[...] * SCALE
    qd = qd_ref[...] * SCALE

    @pl.when(j == 0)
    def _():
        def per_head(tp, cols):
            return jnp.concatenate(
                [jnp.broadcast_to(ps_ref[tp:tp + 1, cols + hk * 128:cols + (hk + 1) * 128], (hrows, 128))
                 for hk in range(C_KV)], axis=0)

        scs = []
        for tp in range(t_len):
            sc = jnp.sum(qc * per_head(tp, 1024), axis=-1, keepdims=True)
            sc = sc - slope * (tc_col - tp).astype(F32)
            scs.append(jnp.where(tp <= tc_col, sc, NEG))
        m = scs[0]
        for sc in scs[1:]:
            m = jnp.maximum(m, sc)
        l = jnp.zeros((QROWS, 1), F32)
        acc = jnp.zeros((QROWS, 128), F32)
        for tp in range(t_len):
            p = jnp.exp(scs[tp] - m)
            l = l + p
            acc = acc + p * per_head(tp, 1280)
        m_ref[...] = m
        l_ref[...] = l
        accc_ref[...] = acc
        c = jnp.zeros((QROWS, 1), F32)
        acc = jnp.zeros((QROWS, 256), F32)
        for tp in range(t_len - 1, -1, -1):
            z = jnp.sum(qd * ps_ref[tp:tp + 1, 2560:2816], axis=-1, keepdims=True)
            earlier = tp < td_col
            sp = jnp.where(earlier, _softplus(z), 0.0)
            a = jnp.exp(jnp.where(earlier, z - sp - c, NEG))
            acc = acc + a * ps_ref[tp:tp + 1, 2816:3072]
            c = c + sp
        c_ref[...] = c
        accd_ref[...] = acc

    qcb = qc.astype(BF16)
    qpos = (past_len + tc_col).astype(F32)
    lane = lax.broadcasted_iota(jnp.int32, (1, PAGE), 1)
    def head_page(ref, hk):
        return ref[pl.ds(hk, PAGE, stride=C_KV), :].astype(BF16)

    ss = []
    for i in range(pp):
        page = npages - 1 - (j * pp + i)
        kpos = (page * PAGE + lane).astype(F32)
        sc = jnp.concatenate([_nt(qcb[hk * hrows:(hk + 1) * hrows], head_page(kc_refs[i], hk))
                              for hk in range(C_KV)], axis=0)
        ss.append(sc - slope * (qpos - kpos))
    s = jnp.concatenate(ss, axis=1)
    m_old = m_ref[...]
    m_new = jnp.maximum(m_old, jnp.max(s, axis=-1, keepdims=True))
    alpha = jnp.exp(m_old - m_new)
    p = jnp.exp(s - m_new)
    l_ref[...] = alpha * l_ref[...] + jnp.sum(p, axis=-1, keepdims=True)
    p = p.astype(BF16)
    pvs = []
    for hk in range(C_KV):
        ph = p[hk * hrows:(hk + 1) * hrows]
        pv = _mm(ph[:, 0:PAGE], head_page(vc_refs[0], hk))
        for i in range(1, pp):
            pv = pv + _mm(ph[:, i * PAGE:(i + 1) * PAGE], head_page(vc_refs[i], hk))
        pvs.append(pv)
    accc_ref[...] = alpha * accc_ref[...] + jnp.concatenate(pvs, axis=0)
    m_ref[...] = m_new

    qdb = qd.astype(BF16)
    c = c_ref[...]
    acc = accd_ref[...]
    for i in range(pp):
        z = _mm(qdb, kd_refs[i][...].astype(BF16))
        tl = _rev_cumsum(_softplus(z), uu_ref[...])
        a = jnp.exp(z - tl - c)
        acc = acc + _nt(a.astype(BF16), vd_refs[i][...].astype(BF16))
        c = c + tl[:, 0:1]
    c_ref[...] = c
    accd_ref[...] = acc

    @pl.when(j == pl.num_programs(1) - 1)
    def _():
        lam = _lambda(lamv_ref[...], lambda_init)
        o = accc_ref[...] / l_ref[...]
        half = t_len * C_GROUP
        for hk in range(C_KV):
            o0 = o[(hk * 2) * half:(hk * 2 + 1) * half, :]
            o1 = o[(hk * 2 + 1) * half:(hk * 2 + 2) * half, :]
            oc_ref[hk * half:(hk + 1) * half, :] = _sub_rms(o0, o1, lam, g_ref[...], lambda_init)
        per = t_len * D_GROUP
        accd = accd_ref[...]
        for hk in range(D_KV):
            od_ref[hk * per:(hk + 1) * per, :] = accd[hk * per:(hk + 1) * per, hk * HD:(hk + 1) * HD]


def _sample_odd(ps, pools, page_base, page_table, lamv, subln_g, lambda_init, t_len=4):
    n, npages = page_table.shape
    pp = PAGES_PER_STEP
    ps3 = ps.reshape(n, t_len, ps.shape[1])
    eye2 = jnp.eye(2, dtype=F32)
    eye4 = jnp.eye(4, dtype=F32)
    qc = ps3[:, :, 0:1024].reshape(n, t_len, C_KV, C_GROUP, 2, HD).transpose(0, 2, 4, 1, 3, 5)
    qc = (qc[:, :, :, :, :, None, :] * eye2[None, None, :, None, None, :, None]).reshape(n, QROWS, 128)
    qd = ps3[:, :, 1536:2560].reshape(n, t_len, D_KV, D_GROUP, HD).transpose(0, 2, 1, 3, 4)
    qd = (qd[:, :, :, :, None, :] * eye4[None, :, None, None, :, None]).reshape(n, QROWS, 256)
    pt = page_table + page_base

    def page_spec(i):
        return pl.BlockSpec((None, 256, PAGE), lambda b, j, pt_ref: (pt_ref[b, npages - 1 - (j * pp + i)], 0, 0))

    per_b = lambda w, rws: pl.BlockSpec((None, rws, w), lambda b, j, pt_ref: (b, 0, 0))
    fixed = lambda shape: pl.BlockSpec(shape, lambda b, j, pt_ref: (0, 0))
    in_specs = [per_b(128, QROWS), per_b(256, QROWS), per_b(ps.shape[1], t_len)]
    ins = [qc, qd, ps3]
    for pool in pools:
        for i in range(pp):
            in_specs.append(page_spec(i))
            ins.append(pool)
    in_specs += [fixed((2 * PAGE, PAGE)), fixed((4, HD)), fixed((1, 2 * HD))]
    ins += [_tri_matrix(PAGE), lamv, subln_g.reshape(1, 2 * HD)]
    oc, od = pl.pallas_call(
        functools.partial(_sample_odd_kernel, npages=npages, t_len=t_len, past_len=npages * PAGE,
                          lambda_init=lambda_init),
        grid_spec=pltpu.PrefetchScalarGridSpec(
            num_scalar_prefetch=1,
            grid=(n, npages // pp),
            in_specs=in_specs,
            out_specs=[per_b(2 * HD, C_KV * t_len * C_GROUP), per_b(HD, D_KV * t_len * D_GROUP)],
            scratch_shapes=[pltpu.VMEM((QROWS, 1), F32), pltpu.VMEM((QROWS, 1), F32), pltpu.VMEM((QROWS, 128), F32),
                            pltpu.VMEM((QROWS, 1), F32), pltpu.VMEM((QROWS, 256), F32)]),
        out_shape=[jax.ShapeDtypeStruct((n, C_KV * t_len * C_GROUP, 2 * HD), F32),
                   jax.ShapeDtypeStruct((n, D_KV * t_len * D_GROUP, HD), F32)],
        compiler_params=_params(("parallel", "arbitrary")),
        name="sample_odd_attn",
    )(pt, *ins)
    oc = oc.reshape(n, C_KV, t_len, C_GROUP, 2 * HD).transpose(0, 2, 1, 3, 4).reshape(n * t_len, C_HEADS * 2 * HD)
    od = od.reshape(n, D_KV, t_len, D_GROUP, HD).transpose(0, 2, 1, 3, 4).reshape(n * t_len, D_HEADS * HD)
    return oc, od


def kernel(x_prompt, x_sample, cache_a1, cache_a2, cache_a3, cache_b, cache_c_k, cache_c_v, cache_d_k, cache_d_v,
           state_conv, page_table, w_in_even, w_out_even, sinks_b, w_in_odd, w_out_odd,
           lam_q1, lam_k1, lam_q2, lam_k2, subln_g, w_ffn_a, conv_w, conv_b, w_ffn_g, w_ffn_down,
           ln_mix_g, ln_mix_b, ln_ffn_g, ln_ffn_b):
    bsz, seq, d = x_prompt.shape
    nb, t_len, _ = x_sample.shape
    assert bsz == 1, "the prompt group is one sequence"
    xp = x_prompt.reshape(seq, d)
    xs = x_sample.reshape(nb * t_len, d)
    slopes = [_slope(h) for h in range(8)]
    a_off = [h * HD for h in range(A_HEADS)]
    b_off = [(h // (B_HEADS // B_KV)) * HD for h in range(B_HEADS)]
    n_phys = cache_c_k.shape[1]
    pools = [c.reshape(c.shape[0] * n_phys, PAGE * C_KV, 2 * HD) for c in (cache_c_k, cache_c_v)]
    pools += [jnp.transpose(c, (0, 1, 3, 4, 2)).reshape(c.shape[0] * n_phys, D_KV * HD, PAGE)
              for c in (cache_d_k, cache_d_v)]
    even_p, even_s, odd_p, odd_s, conv_p, conv_s = [], [], [], [], [], []
    for layer in range(DEPTH):
        i = layer // 2
        if layer % 2 == 0:
            w_in = jnp.pad(w_in_even[i].astype(BF16), ((0, 0), (0, EVEN_PAD - EVEN_IN)))
            w_out = w_out_even[i].astype(BF16)
            pp = _matmul(xp, w_in, 512, 1408)
            ps = _matmul(xs, w_in, 512, 1408)
            prev = None
            for g, (_, dil) in enumerate(A_PATTERNS):
                last = g == len(A_PATTERNS) - 1
                res = _band_prompt(pp, dil, g * 512, 1536 + g * 512, 3072 + g * 512, 512, a_off, slopes,
                                   prev=prev, want_lse=not last)
                prev = None if last else res
            oa_p = res
            ob_p = _band_prompt(pp, 1, 4608, 5120, 5248, 128, b_off, slopes, sinks=sinks_b[i].reshape(B_HEADS))
            oa_s, ob_s = _sample_even(ps, cache_a1, cache_a2, cache_a3, cache_b, i, sinks_b[i].reshape(B_HEADS), t_len)
            w1, w2 = w_out[:512], w_out[512:]
            rows_p, rows_s = [], []
            for g, (win, _) in enumerate(A_PATTERNS):
                w = min(win, seq)
                kv = jnp.stack([pp[seq - w:, 1536 + g * 512:2048 + g * 512].reshape(w, A_HEADS, HD),
                                pp[seq - w:, 3072 + g * 512:3584 + g * 512].reshape(w, A_HEADS, HD)], axis=1)
                rows_p.append(kv[None])
                rows_s.append(jnp.stack([ps[:, 1536 + g * 512:2048 + g * 512].reshape(nb, t_len, A_HEADS, HD),
                                         ps[:, 3072 + g * 512:3584 + g * 512].reshape(nb, t_len, A_HEADS, HD)], axis=2))
            w = min(128, seq)
            rows_p.append(jnp.stack([pp[seq - w:, 5120:5248].reshape(w, B_KV, HD),
                                     pp[seq - w:, 5248:5376].reshape(w, B_KV, HD)], axis=1)[None])
            rows_s.append(jnp.stack([ps[:, 5120:5248].reshape(nb, t_len, B_KV, HD),
                                     ps[:, 5248:5376].reshape(nb, t_len, B_KV, HD)], axis=2))
            even_p.append(rows_p)
            even_s.append(rows_s)
            mix_p, mix_s = (oa_p, ob_p), (oa_s, ob_s)
        else:
            lambda_init = 0.8 - 0.6 * math.exp(-0.3 * layer)
            lamv = jnp.stack([lam_q1[i], lam_k1[i], lam_q2[i], lam_k2[i]])
            w_in = w_in_odd[i].astype(BF16)
            w_out = w_out_odd[i].astype(BF16)
            pp = _matmul(xp, w_in, 512, 1024)
            ps = _matmul(xs, w_in, 512, 1024)
            oc_p = _diff_prompt(pp, lamv, subln_g[i], lambda_init)
            od_p = _sb_prompt(pp)
            oc_s, od_s = _sample_odd(ps, pools, i * n_phys, page_table, lamv, subln_g[i], lambda_init, t_len)
            w1, w2 = w_out[:1024], w_out[1024:]
            odd_p.append((pp[:, 1024:1280].reshape(1, seq, C_KV, 2 * HD), pp[:, 1280:1536].reshape(1, seq, C_KV, 2 * HD),
                          pp[:, 2560:2816].reshape(1, seq, D_KV, HD), pp[:, 2816:3072].reshape(1, seq, D_KV, HD)))
            odd_s.append((ps[:, 1024:1280].reshape(nb, t_len, C_KV, 2 * HD), ps[:, 1280:1536].reshape(nb, t_len, C_KV, 2 * HD),
                          ps[:, 2560:2816].reshape(nb, t_len, D_KV, HD), ps[:, 2816:3072].reshape(nb, t_len, D_KV, HD)))
            mix_p, mix_s = (oc_p, od_p), (oc_s, od_s)
        xp = _outproj_ln(mix_p[0], mix_p[1], w1, w2, xp, ln_mix_g[layer], ln_mix_b[layer])
        xs = _outproj_ln(mix_s[0], mix_s[1], w1, w2, xs, ln_mix_g[layer], ln_mix_b[layer])
        wa, wg, wd = w_ffn_a[layer].astype(BF16), w_ffn_g[layer].astype(BF16), w_ffn_down[layer].astype(BF16)
        ffn_args = (wa, wg, wd, conv_w[layer], conv_b[layer], ln_ffn_g[layer], ln_ffn_b[layer])
        xp, tail_p = _ffn(xp, *ffn_args)
        st = state_conv[layer]
        zero = jnp.zeros((nb, t_len, D_FF), F32)
        p1 = zero.at[:, 0].set(st[:, 1]).reshape(nb * t_len, D_FF)
        p2 = zero.at[:, 0].set(st[:, 0]).at[:, 1].set(st[:, 1]).reshape(nb * t_len, D_FF)
        xs, tail_s = _ffn(xs, *ffn_args, prev=(p1, p2), t_len=t_len)
        conv_p.append(tail_p[-2:][None])
        conv_s.append(tail_s.reshape(nb, t_len, D_FF)[:, t_len - 2:])
    stack = lambda per_layer, j: jnp.stack([entry[j] for entry in per_layer])
    outs = [xp.reshape(1, seq, d), xs.reshape(nb, t_len, d)]
    for j in range(4):
        outs += [stack(even_p, j), stack(even_s, j)]
    for j in range(4):
        outs += [stack(odd_p, j), stack(odd_s, j)]
    outs += [jnp.stack(conv_p), jnp.stack(conv_s)]
    return tuple(outs)
```

```python
import functools
import math

import numpy as np
import jax
import jax.numpy as jnp
from jax import lax
from jax.experimental import pallas as pl
from jax.experimental.pallas import tpu as pltpu

F32 = jnp.float32
BF16 = jnp.bfloat16

D_MODEL = 2048
DEPTH = 2
PAGE = 128
HD = 64
BLK = 128
A_PATTERNS = ((128, 1), (512, 4), (2048, 16))
A_HEADS = 8
B_HEADS = 8
B_KV = 2
C_HEADS = 8
C_KV = 2
D_HEADS = 16
D_KV = 4
D_FF = 5632
LN_EPS = 1e-5
NEG = -1e30
ALPHA = (2 * DEPTH) ** 0.25
EVEN_IN = 5376
EVEN_PAD = 5632
ODD_IN = 3072
SCALE = HD ** -0.5

VMEM_LIMIT = 56 * 1024 * 1024


def _params(sem):
    return pltpu.CompilerParams(dimension_semantics=sem, vmem_limit_bytes=VMEM_LIMIT)


def _nt(a, b):
    return lax.dot_general(a, b, (((1,), (1,)), ((), ())), preferred_element_type=F32)


def _mm(a, b):
    return jnp.dot(a, b, preferred_element_type=F32)


def _slope(h, n=8):
    return 2.0 ** (-8.0 * (h + 1) / n)


def _matmul_kernel(x_ref, w_ref, o_ref, xb_ref):
    @pl.when(pl.program_id(1) == 0)
    def _():
        xb_ref[...] = x_ref[...].astype(BF16)

    o_ref[...] = _mm(xb_ref[...], w_ref[...])


def _matmul(x, w, tm, tn):
    m, k = x.shape
    n = w.shape[1]
    tm = min(tm, m)
    return pl.pallas_call(
        _matmul_kernel,
        grid=(m // tm, n // tn),
        in_specs=[pl.BlockSpec((tm, k), lambda i, j: (i, 0)),
                  pl.BlockSpec((k, tn), lambda i, j: (0, j))],
        out_specs=pl.BlockSpec((tm, tn), lambda i, j: (i, j)),
        out_shape=jax.ShapeDtypeStruct((m, n), F32),
        scratch_shapes=[pltpu.VMEM((tm, k), BF16)],
        compiler_params=_params(("parallel", "arbitrary")),
        name="proj_matmul",
    )(x, w)


def _layer_norm_rows(r, g, b):
    mu = jnp.mean(r, axis=-1, keepdims=True)
    d = r - mu
    var = jnp.mean(d * d, axis=-1, keepdims=True)
    return d * lax.rsqrt(var + LN_EPS) * g + b


def _outproj_ln_kernel(m1_ref, m2_ref, w1_ref, w2_ref, x_ref, g_ref, b_ref, o_ref):
    y = _mm(m1_ref[...].astype(BF16), w1_ref[...]) + _mm(m2_ref[...].astype(BF16), w2_ref[...])
    o_ref[...] = _layer_norm_rows(ALPHA * x_ref[...] + y, g_ref[...], b_ref[...])


def _outproj_ln(m1, m2, w1, w2, x, g, b, tm=256):
    m, d = x.shape
    k1, k2 = m1.shape[1], m2.shape[1]
    tm = min(tm, m)
    row = lambda i: (i, 0)
    fixed = lambda i: (0, 0)
    return pl.pallas_call(
        _outproj_ln_kernel,
        grid=(m // tm,),
        in_specs=[pl.BlockSpec((tm, k1), row), pl.BlockSpec((tm, k2), row),
                  pl.BlockSpec((k1, d), fixed), pl.BlockSpec((k2, d), fixed),
                  pl.BlockSpec((tm, d), row), pl.BlockSpec((1, d), fixed), pl.BlockSpec((1, d), fixed)],
        out_specs=pl.BlockSpec((tm, d), row),
        out_shape=jax.ShapeDtypeStruct((m, d), F32),
        compiler_params=_params(("parallel",)),
        name="outproj_ln",
    )(m1, m2, w1, w2, x, g.reshape(1, d), b.reshape(1, d))


FFN_HALO = 16


def _ffn_kernel(*refs, tm, sample, t_len):
    if sample:
        x_ref, p1_ref, p2_ref, wa_ref, wg_ref, wd_ref, cw_ref, cb_ref, g_ref, b_ref, o_ref, tail_ref, xb_ref, acc_ref = refs
    else:
        x_ref, xh_ref, wa_ref, wg_ref, wd_ref, cw_ref, cb_ref, g_ref, b_ref, o_ref, tail_ref, xb_ref, acc_ref = refs
    i = pl.program_id(0)
    f = pl.program_id(1)

    @pl.when(f == 0)
    def _():
        xb_ref[FFN_HALO:, :] = x_ref[...].astype(BF16)
        if sample:
            xb_ref[:FFN_HALO, :] = jnp.zeros((FFN_HALO, x_ref.shape[1]), BF16)
        else:
            halo = jnp.where(i > 0, xh_ref[...], 0.0)
            xb_ref[:FFN_HALO, :] = halo.astype(BF16)
        acc_ref[...] = jnp.zeros_like(acc_ref)

    a_ext = _mm(xb_ref[...], wa_ref[...])
    gate = _mm(xb_ref[FFN_HALO:, :], wg_ref[...])
    a = a_ext[FFN_HALO:, :]
    a1 = pltpu.roll(a_ext, 1, 0)[FFN_HALO:, :]
    a2 = pltpu.roll(a_ext, 2, 0)[FFN_HALO:, :]
    if sample:
        t = lax.broadcasted_iota(jnp.int32, a.shape, 0) % t_len
        a1 = jnp.where(t == 0, p1_ref[...], a1)
        a2 = jnp.where(t < 2, p2_ref[...], a2)
    cw = cw_ref[...]
    c = a2 * cw[0:1, :] + a1 * cw[1:2, :] + a * cw[2:3, :] + cb_ref[...]
    h = (c * jax.nn.sigmoid(c)) * gate
    acc_ref[...] += _mm(h.astype(BF16), wd_ref[...])
    tail_ref[...] = a[tm - tail_ref.shape[0]:, :]

    @pl.when(f == pl.num_programs(1) - 1)
    def _():
        o_ref[...] = _layer_norm_rows(ALPHA * x_ref[...] + acc_ref[...], g_ref[...], b_ref[...])


def _ffn(x, wa, wg, wd, cw, cb, g, b, prev=None, t_len=4, tm=512, tf=512):
    m, d = x.shape
    dff = wa.shape[1]
    tm = min(tm, m)
    sample = prev is not None
    row = lambda i, f: (i, 0)
    col = lambda i, f: (0, f)
    wcol = pl.BlockSpec((d, tf), col)
    ins = [x]
    specs = [pl.BlockSpec((tm, d), row)]
    if sample:
        ins += list(prev)
        specs += [pl.BlockSpec((tm, tf), lambda i, f: (i, f))] * 2
        tail_rows = tm
        tail_spec = pl.BlockSpec((tm, tf), lambda i, f: (i, f))
        tail_shape = (m, dff)
    else:
        ins += [x]
        hb = tm // FFN_HALO
        specs += [pl.BlockSpec((FFN_HALO, d), lambda i, f: (jnp.maximum(i * hb - 1, 0), 0))]
        tail_rows = 8
        tail_spec = pl.BlockSpec((8, tf), lambda i, f: (i, f))
        tail_shape = (8 * (m // tm), dff)
    ins += [wa, wg, wd, cw, cb.reshape(1, dff), g.reshape(1, d), b.reshape(1, d)]
    specs += [wcol, wcol, pl.BlockSpec((tf, d), lambda i, f: (f, 0)),
              pl.BlockSpec((3, tf), col), pl.BlockSpec((1, tf), col),
              pl.BlockSpec((1, d), lambda i, f: (0, 0)), pl.BlockSpec((1, d), lambda i, f: (0, 0))]
    return pl.pallas_call(
        functools.partial(_ffn_kernel, tm=tm, sample=sample, t_len=t_len),
        grid=(m // tm, dff // tf),
        in_specs=specs,
        out_specs=[pl.BlockSpec((tm, d), row), tail_spec],
        out_shape=[jax.ShapeDtypeStruct((m, d), F32), jax.ShapeDtypeStruct(tail_shape, F32)],
        scratch_shapes=[pltpu.VMEM((tm + FFN_HALO, d), BF16), pltpu.VMEM((tm, d), F32)],
        compiler_params=_params(("arbitrary", "arbitrary")),
        name="conv_ffn_sample" if sample else "conv_ffn_prompt",
    )(*ins)


def _band_prompt_kernel(*refs, dil, kv_off, slopes, has_sink, has_prev, want_lse):
    it = iter(refs)
    q_ref, kc_ref, kp_ref, vc_ref, vp_ref = (next(it) for _ in range(5))
    sink_ref = next(it) if has_sink else None
    op_ref, lp_ref = (next(it), next(it)) if has_prev else (None, None)
    o_ref = next(it)
    l_ref = next(it) if want_lse else None
    b = pl.program_id(1)
    qi = lax.broadcasted_iota(jnp.int32, (BLK, BLK), 0)
    kj = lax.broadcasted_iota(jnp.int32, (BLK, BLK), 1)
    mask_c = kj <= qi
    mask_p = jnp.logical_and(kj >= qi, b > 0)
    dist_c = (qi - kj).astype(F32) * float(dil)
    dist_p = (BLK + qi - kj).astype(F32) * float(dil)
    for h in range(len(kv_off)):
        hs = slice(h * HD, (h + 1) * HD)
        ks = slice(kv_off[h], kv_off[h] + HD)
        q = q_ref[:, hs].astype(BF16)
        s_c = _nt(q, kc_ref[:, ks].astype(BF16)) * SCALE - slopes[h] * dist_c
        s_p = _nt(q, kp_ref[:, ks].astype(BF16)) * SCALE - slopes[h] * dist_p
        s_c = jnp.where(mask_c, s_c, NEG)
        s_p = jnp.where(mask_p, s_p, NEG)
        m = jnp.maximum(jnp.max(s_c, axis=-1, keepdims=True), jnp.max(s_p, axis=-1, keepdims=True))
        p_c = jnp.exp(s_c - m)
        p_p = jnp.exp(s_p - m)
        l = jnp.sum(p_c, axis=-1, keepdims=True) + jnp.sum(p_p, axis=-1, keepdims=True)
        if has_sink:
            l = l + jnp.exp(sink_ref[h] - m)
        o = _mm(p_c.astype(BF16), vc_ref[:, ks].astype(BF16)) + _mm(p_p.astype(BF16), vp_ref[:, ks].astype(BF16))
        o = o / l
        lse = m + jnp.log(l)
        if has_prev:
            lse_prev = lp_ref[:, hs]
            lse_new = jnp.logaddexp(lse_prev, lse)
            o = op_ref[:, hs] * jnp.exp(lse_prev - lse_new) + o * jnp.exp(lse - lse_new)
            lse = lse_new
        o_ref[:, hs] = o
        if want_lse:
            l_ref[:, hs] = jnp.broadcast_to(lse, (BLK, HD))


def _band_prompt(proj, dil, q_col, k_col, v_col, kv_w, kv_off, slopes, sinks=None, prev=None, want_lse=False):
    s, w = proj.shape
    length = s // dil
    nb = length // BLK
    view = proj.reshape(length, dil * w)
    nh = len(kv_off)
    qw = nh * HD
    qspec = pl.BlockSpec((BLK, qw), lambda r, b: (b, r * (w // qw) + q_col // qw))
    kc = pl.BlockSpec((BLK, kv_w), lambda r, b: (b, r * (w // kv_w) + k_col // kv_w))
    kp = pl.BlockSpec((BLK, kv_w), lambda r, b: (jnp.maximum(b - 1, 0), r * (w // kv_w) + k_col // kv_w))
    vc = pl.BlockSpec((BLK, kv_w), lambda r, b: (b, r * (w // kv_w) + v_col // kv_w))
    vp = pl.BlockSpec((BLK, kv_w), lambda r, b: (jnp.maximum(b - 1, 0), r * (w // kv_w) + v_col // kv_w))
    ospec = pl.BlockSpec((BLK, qw), lambda r, b: (b, r))
    ins = [view, view, view, view, view]
    specs = [qspec, kc, kp, vc, vp]
    if sinks is not None:
        ins.append(sinks)
        specs.append(pl.BlockSpec(memory_space=pltpu.SMEM))
    if prev is not None:
        ins += [prev[0].reshape(length, dil * qw), prev[1].reshape(length, dil * qw)]
        specs += [ospec, ospec]
    oshape = jax.ShapeDtypeStruct((length, dil * qw), F32)
    outs = pl.pallas_call(
        functools.partial(_band_prompt_kernel, dil=dil, kv_off=tuple(kv_off), slopes=tuple(slopes),
                          has_sink=sinks is not None, has_prev=prev is not None, want_lse=want_lse),
        grid=(dil, nb),
        in_specs=specs,
        out_specs=[ospec, ospec] if want_lse else ospec,
        out_shape=[oshape, oshape] if want_lse else oshape,
        compiler_params=_params(("parallel", "parallel")),
        name=f"band_prompt_d{dil}",
    )(*ins)
    if want_lse:
        return outs[0].reshape(s, qw), outs[1].reshape(s, qw)
    return outs.reshape(s, qw)


def _col_consts(values, rows):
    r = lax.broadcasted_iota(jnp.int32, (rows, 1), 0)
    col = jnp.zeros((rows, 1), F32)
    for h, v in enumerate(values):
        col = jnp.where(r == h, v, col)
    return col


def _sample_even_kernel(ps_ref, a1_ref, a2_ref, a3_ref, cb_ref, fold_ref, sink_ref, oa_ref, ob_ref, *, t_len):
    nh = 8
    nr = t_len * nh
    row = lax.broadcasted_iota(jnp.int32, (nr, 1), 0)
    t_col = row // nh
    h_col = row % nh
    hm = (lax.broadcasted_iota(jnp.int32, (nr, nh * HD), 1) // HD == h_col).astype(F32)
    slope = jnp.zeros((nr, 1), F32)
    sink_col = jnp.zeros((nr, 1), F32)
    for h in range(nh):
        slope = jnp.where(h_col == h, _slope(h), slope)
        sink_col = jnp.where(h_col == h, sink_ref[h], sink_col)

    def q_rows(col):
        return jnp.concatenate([jnp.broadcast_to(ps_ref[t:t + 1, col:col + 512], (nh, 512))
                                for t in range(t_len)], axis=0) * hm

    def attend(qk, q_new, kt_ref_slice, vt_ref_slice, k_col, v_col, width, dil, sinks):
        kt = kt_ref_slice.astype(BF16)
        length = kt.shape[1]
        dd = length + t_col - lax.broadcasted_iota(jnp.int32, (nr, length), 1)
        valid = jnp.logical_and((dd & (dil - 1)) == 0, dd <= dil * BLK)
        s = jnp.where(valid, _mm(qk, kt) * SCALE - slope * dd.astype(F32), NEG)
        m = jnp.max(s, axis=-1, keepdims=True)
        new = []
        for tp in range(t_len):
            dn = t_col - tp
            ok = jnp.logical_and(dn >= 0, (dn & (dil - 1)) == 0)
            sc = jnp.sum(q_new * ps_ref[tp:tp + 1, k_col:k_col + width], axis=-1, keepdims=True) * SCALE
            sc = jnp.where(ok, sc - slope * dn.astype(F32), NEG)
            new.append(sc)
            m = jnp.maximum(m, sc)
        p = jnp.exp(s - m)
        l = jnp.sum(p, axis=-1, keepdims=True)
        o = _nt(p.astype(BF16), vt_ref_slice.astype(BF16))
        for tp in range(t_len):
            pn = jnp.exp(new[tp] - m)
            l = l + pn
            o = o + pn * ps_ref[tp:tp + 1, v_col:v_col + width]
        if sinks:
            l = l + jnp.exp(sink_col - m)
        return o / l, m + jnp.log(l)

    def head_rows(o_wide):
        return jnp.sum((o_wide * hm).reshape(t_len, nh, nh * HD), axis=1)

    outs, lses = [], []
    for g, (buf, (_, dil)) in enumerate(zip((a1_ref, a2_ref, a3_ref), A_PATTERNS)):
        q = q_rows(g * 512)
        o, lse = attend(q.astype(BF16), q, buf[0:512, :], buf[512:1024, :],
                        1536 + g * 512, 3072 + g * 512, 512, dil, False)
        outs.append(o)
        lses.append(lse)
    mx = jnp.maximum(jnp.maximum(lses[0], lses[1]), lses[2])
    es = [jnp.exp(l_ - mx) for l_ in lses]
    merged = (es[0] * outs[0] + es[1] * outs[1] + es[2] * outs[2]) / (es[0] + es[1] + es[2])
    oa_ref[...] = head_rows(merged)

    qf = _mm(q_rows(4608).astype(BF16), fold_ref[...])
    o, _ = attend(qf.astype(BF16), qf, cb_ref[0:128, :], cb_ref[128:256, :], 5120, 5248, 128, 1, True)
    o_sw = pltpu.roll(o, HD, 1)
    o_sel = jnp.where((h_col // 4) == (h_col % 2), o, o_sw)
    ob_ref[...] = head_rows(jnp.concatenate([o_sel] * 4, axis=1))


def _fold_matrix():
    c = np.arange(512)[:, None]
    l = np.arange(128)[None, :]
    return jnp.asarray(((c % HD == l % HD) & (l // HD == (c // HD) // 4)).astype(np.float32), BF16)


def _sample_even(ps, a1, a2, a3, cb, layer, sinks, t_len=4):
    nl, n = a1.shape[0], a1.shape[1]
    ps3 = ps.reshape(n, t_len, ps.shape[1])

    def rows_last(c):
        return jnp.transpose(c, (0, 1, 3, 4, 5, 2)).reshape(nl * n, -1, c.shape[2])

    a1v, a2v, a3v, cbv = rows_last(a1), rows_last(a2), rows_last(a3), rows_last(cb)
    b3 = lambda arr: pl.BlockSpec((None,) + arr.shape[1:], lambda i: (layer * n + i, 0, 0))
    ospec = pl.BlockSpec((None, t_len, 512), lambda i: (i, 0, 0))
    oa, ob = pl.pallas_call(
        functools.partial(_sample_even_kernel, t_len=t_len),
        grid=(n,),
        in_specs=[pl.BlockSpec((None, t_len, ps.shape[1]), lambda i: (i, 0, 0)),
                  b3(a1v), b3(a2v), b3(a3v), b3(cbv),
                  pl.BlockSpec((512, 128), lambda i: (0, 0)),
                  pl.BlockSpec(memory_space=pltpu.SMEM)],
        out_specs=[ospec, ospec],
        out_shape=[jax.ShapeDtypeStruct((n, t_len, 512), F32)] * 2,
        compiler_params=_params(("parallel",)),
        name="sample_even_attn",
    )(ps3, a1v, a2v, a3v, cbv, _fold_matrix(), sinks)
    return oa.reshape(n * t_len, 512), ob.reshape(n * t_len, 512)


C_GROUP = C_HEADS // C_KV
D_GROUP = D_HEADS // D_KV


def _lambda(lamv, lambda_init):
    s1 = jnp.sum(lamv[0:1, :] * lamv[1:2, :], axis=-1, keepdims=True)
    s2 = jnp.sum(lamv[2:3, :] * lamv[3:4, :], axis=-1, keepdims=True)
    return jnp.exp(s1) - jnp.exp(s2) + lambda_init


def _sub_rms(o0, o1, lam, g, lambda_init):
    d = o0 - lam * o1
    return d * lax.rsqrt(jnp.mean(d * d, axis=-1, keepdims=True) + LN_EPS) * g * (1.0 - lambda_init)


def _softplus(z):
    return jnp.maximum(z, 0.0) + jnp.log(1.0 + jnp.exp(-jnp.abs(z)))


def _tri_matrix(n):
    u = (np.arange(n)[:, None] >= np.arange(n)[None, :]).astype(np.float32)
    return jnp.asarray(np.concatenate([u, u], axis=0), BF16)


def _rev_cumsum(sp, uu):
    hi = sp.astype(BF16)
    lo = (sp - hi.astype(F32)).astype(BF16)
    return _mm(jnp.concatenate([hi, lo], axis=1), uu)


def _matmul_t_kernel(wt_ref, x_ref, o_ref):
    o_ref[...] = _nt(wt_ref[...], x_ref[...].astype(BF16))


def _matmul_t(wt, x, tm=512):
    n, k = wt.shape
    m = x.shape[0]
    tm = min(tm, m)
    return pl.pallas_call(
        _matmul_t_kernel,
        grid=(m // tm,),
        in_specs=[pl.BlockSpec((n, k), lambda i: (0, 0)), pl.BlockSpec((tm, k), lambda i: (i, 0))],
        out_specs=pl.BlockSpec((n, tm), lambda i: (0, i)),
        out_shape=jax.ShapeDtypeStruct((n, m), F32),
        compiler_params=_params(("parallel",)),
        name="proj_matmul_t",
    )(wt, x)


def _diff_prompt_kernel(q_ref, k_ref, vt_ref, lamv_ref, g_ref, o_ref, b0_ref, m_ref, l_ref, acc_ref, *, tq, lambda_init):
    hk = pl.program_id(0)
    i = pl.program_id(1)
    rows = C_GROUP * tq
    lane = lax.broadcasted_iota(jnp.int32, (1, rows), 1)
    slope = jnp.zeros((1, rows), F32)
    for g in range(C_GROUP):
        slope = jnp.where(lane // tq == g, _slope(g), slope)
    slope = slope * jnp.where(hk == 0, 1.0, 2.0 ** (-C_GROUP)).astype(F32)
    kl = lax.broadcasted_iota(jnp.int32, (tq, rows), 0)
    ql = lax.broadcasted_iota(jnp.int32, (tq, rows), 1) % tq
    b0_ref[...] = slope * (kl - ql).astype(F32)
    qs = []
    for m in range(2):
        parts = [q_ref[:, (g * 2 + m) * HD:(g * 2 + m + 1) * HD] for g in range(C_GROUP)]
        qs.append((jnp.concatenate(parts, axis=0) * SCALE).astype(BF16))
    m_ref[...] = jnp.full(m_ref.shape, NEG, F32)
    l_ref[...] = jnp.zeros_like(l_ref)
    acc_ref[...] = jnp.zeros_like(acc_ref)

    def block(j, diagonal):
        start = pl.multiple_of(j * tq, tq)
        k = k_ref[pl.ds(start, tq), :].astype(BF16)
        vt = vt_ref[:, pl.ds(start, tq)].astype(BF16)
        cj = -slope * ((i - j) * tq).astype(F32)
        for m in range(2):
            s = _nt(k[:, m * HD:(m + 1) * HD], qs[m]) + b0_ref[...]
            if diagonal:
                s = jnp.where(kl <= ql, s, NEG)
            m_old = m_ref[m]
            m_new = jnp.maximum(m_old, jnp.max(s, axis=0, keepdims=True) + cj)
            alpha = jnp.exp(m_old - m_new)
            p = jnp.exp(s + (cj - m_new))
            l_ref[m] = alpha * l_ref[m] + jnp.sum(p, axis=0, keepdims=True)
            acc_ref[m] = alpha * acc_ref[m] + _mm(vt, p.astype(BF16))
            m_ref[m] = m_new

    def off_diagonal(j, carry):
        block(j, False)
        return carry

    lax.fori_loop(0, i, off_diagonal, 0)
    block(i, True)
    lam = _lambda(lamv_ref[...], lambda_init)
    d = acc_ref[0] / l_ref[0] - lam * (acc_ref[1] / l_ref[1])
    d = d * lax.rsqrt(jnp.mean(d * d, axis=0, keepdims=True) + LN_EPS) * g_ref[...] * (1.0 - lambda_init)
    for g in range(C_GROUP):
        o_ref[:, g * 2 * HD:(g + 1) * 2 * HD] = d[:, g * tq:(g + 1) * tq].T


def _diff_prompt(proj, vt, lamv, subln_g, lambda_init, tq=256):
    s = proj.shape[0]
    rows = C_GROUP * tq
    return pl.pallas_call(
        functools.partial(_diff_prompt_kernel, tq=tq, lambda_init=lambda_init),
        grid=(C_KV, s // tq),
        in_specs=[pl.BlockSpec((tq, 512), lambda h, i: (i, h)),
                  pl.BlockSpec((s, 128), lambda h, i: (0, 8 + h)),
                  pl.BlockSpec((2 * HD, s), lambda h, i: (h, 0)),
                  pl.BlockSpec((4, HD), lambda h, i: (0, 0)),
                  pl.BlockSpec((2 * HD, 1), lambda h, i: (0, 0))],
        out_specs=pl.BlockSpec((tq, 512), lambda h, i: (i, h)),
        out_shape=jax.ShapeDtypeStruct((s, C_HEADS * 2 * HD), F32),
        scratch_shapes=[pltpu.VMEM((tq, rows), F32), pltpu.VMEM((2, 1, rows), F32), pltpu.VMEM((2, 1, rows), F32),
                        pltpu.VMEM((2, 2 * HD, rows), F32)],
        compiler_params=_params(("parallel", "arbitrary")),
        name="diff_attn_prompt",
    )(proj, proj, vt, lamv, subln_g.reshape(2 * HD, 1))


def _tri_matrix_t(n):
    u = (np.arange(n)[None, :] >= np.arange(n)[:, None]).astype(np.float32)
    return jnp.asarray(np.concatenate([u, u], axis=1), BF16)


def _sb_prompt_kernel(q_ref, k_ref, vt_ref, ut_ref, o_ref, c_ref, acc_ref, *, tq):
    i = pl.program_id(1)
    rows = D_GROUP * tq
    kl = lax.broadcasted_iota(jnp.int32, (tq, rows), 0)
    ql = lax.broadcasted_iota(jnp.int32, (tq, rows), 1) % tq
    qs = []
    for hl in range(2):
        parts = [q_ref[:, (hl * D_GROUP + g) * HD:(hl * D_GROUP + g + 1) * HD] for g in range(D_GROUP)]
        qs.append((jnp.concatenate(parts, axis=0) * SCALE).astype(BF16))
    c_ref[...] = jnp.zeros_like(c_ref)
    acc_ref[...] = jnp.zeros_like(acc_ref)

    def block(j, diagonal):
        start = pl.multiple_of(j * tq, tq)
        for hl in range(2):
            k = k_ref[pl.ds(start, tq), hl * HD:(hl + 1) * HD].astype(BF16)
            vt = vt_ref[hl * HD:(hl + 1) * HD, pl.ds(start, tq)].astype(BF16)
            z = _nt(k, qs[hl])
            sp = _softplus(z)
            if diagonal:
                sp = jnp.where(kl < ql, sp, 0.0)
            hi = sp.astype(BF16)
            lo = (sp - hi.astype(F32)).astype(BF16)
            tl = _mm(ut_ref[...], jnp.concatenate([hi, lo], axis=0))
            c = c_ref[hl]
            e = z - tl - c
            if diagonal:
                e = jnp.where(kl < ql, e, NEG)
            acc_ref[hl * HD:(hl + 1) * HD, :] += _mm(vt, jnp.exp(e).astype(BF16))
            c_ref[hl] = c + tl[0:1, :]

    block(i, True)

    def older(it, carry):
        block(i - 1 - it, False)
        return carry

    lax.fori_loop(0, i, older, 0)
    for g in range(D_GROUP):
        t = acc_ref[:, g * tq:(g + 1) * tq].T
        for hl in range(2):
            o_ref[:, (hl * D_GROUP + g) * HD:(hl * D_GROUP + g + 1) * HD] = t[:, hl * HD:(hl + 1) * HD]


def _sb_prompt(proj, vt, tq=256):
    s = proj.shape[0]
    rows = D_GROUP * tq
    return pl.pallas_call(
        functools.partial(_sb_prompt_kernel, tq=tq),
        grid=(D_KV // 2, s // tq),
        in_specs=[pl.BlockSpec((tq, 512), lambda h, i: (i, 3 + h)),
                  pl.BlockSpec((s, 128), lambda h, i: (0, 20 + h)),
                  pl.BlockSpec((2 * HD, s), lambda h, i: (2 + h, 0)),
                  pl.BlockSpec((tq, 2 * tq), lambda h, i: (0, 0))],
        out_specs=pl.BlockSpec((tq, 512), lambda h, i: (i, h)),
        out_shape=jax.ShapeDtypeStruct((s, D_HEADS * HD), F32),
        scratch_shapes=[pltpu.VMEM((2, 1, rows), F32), pltpu.VMEM((2 * HD, rows), F32)],
        compiler_params=_params(("parallel", "arbitrary")),
        name="sb_attn_prompt",
    )(proj, proj, vt, _tri_matrix_t(tq))


PAGES_PER_STEP = 8
QROWS = 64


def _sample_odd_kernel(pt_ref, qc_ref, qd_ref, ps_ref, *refs, npages, t_len, past_len, lambda_init):
    pp = PAGES_PER_STEP
    hrows = QROWS // C_KV
    kc_refs, vc_refs = refs[0:pp], refs[pp:2 * pp]
    kd_refs, vd_refs = refs[2 * pp:3 * pp], refs[3 * pp:4 * pp]
    uu_ref, lamv_ref, g_ref, oc_ref, od_ref, m_ref, l_ref, accc_ref, c_ref, accd_ref = refs[4 * pp:]
    j = pl.program_id(1)
    r = lax.broadcasted_iota(jnp.int32, (QROWS, 1), 0)
    tc_col = (r // C_GROUP) % t_len
    td_col = (r // D_GROUP) % t_len
    head_c = (r // (2 * t_len * C_GROUP)) * C_GROUP + r % C_GROUP
    slope = jnp.zeros((QROWS, 1), F32)
    for h in range(C_HEADS):
        slope = jnp.where(head_c == h, _slope(h), slope)
    qc = qc_ref[...] * SCALE
    qd = qd_ref[...] * SCALE

    @pl.when(j == 0)
    def _():
        def per_head(tp, cols):
            return jnp.concatenate(
                [jnp.broadcast_to(ps_ref[tp:tp + 1, cols + hk * 128:cols + (hk + 1) * 128], (hrows, 128))
                 for hk in range(C_KV)], axis=0)

        scs = []
        for tp in range(t_len):
            sc = jnp.sum(qc * per_head(tp, 1024), axis=-1, keepdims=True)
            sc = sc - slope * (tc_col - tp).astype(F32)
            scs.append(jnp.where(tp <= tc_col, sc, NEG))
        m = scs[0]
        for sc in scs[1:]:
            m = jnp.maximum(m, sc)
        l = jnp.zeros((QROWS, 1), F32)
        acc = jnp.zeros((QROWS, 128), F32)
        for tp in range(t_len):
            p = jnp.exp(scs[tp] - m)
            l = l + p
            acc = acc + p * per_head(tp, 1280)
        m_ref[...] = m
        l_ref[...] = l
        accc_ref[...] = acc
        c = jnp.zeros((QROWS, 1), F32)
        acc = jnp.zeros((QROWS, 256), F32)
        for tp in range(t_len - 1, -1, -1):
            z = jnp.sum(qd * ps_ref[tp:tp + 1, 2560:2816], axis=-1, keepdims=True)
            earlier = tp < td_col
            sp = jnp.where(earlier, _softplus(z), 0.0)
            a = jnp.exp(jnp.where(earlier, z - sp - c, NEG))
            acc = acc + a * ps_ref[tp:tp + 1, 2816:3072]
            c = c + sp
        c_ref[...] = c
        accd_ref[...] = acc

    qcb = qc.astype(BF16)
    qpos = (past_len + tc_col).astype(F32)
    col = lax.broadcasted_iota(jnp.int32, (1, PAGE * C_KV), 1)
    own_head = (col % C_KV) == (r // hrows)
    ckey = col // C_KV
    ss = []
    for i in range(pp):
        page = npages - 1 - (j * pp + i)
        kpos = (page * PAGE + ckey).astype(F32)
        sc = _nt(qcb, kc_refs[i][...].astype(BF16))
        ss.append(jnp.where(own_head, sc - slope * (qpos - kpos), NEG))
    s = jnp.concatenate(ss, axis=1)
    m_old = m_ref[...]
    m_new = jnp.maximum(m_old, jnp.max(s, axis=-1, keepdims=True))
    alpha = jnp.exp(m_old - m_new)
    p = jnp.exp(s - m_new)
    l_ref[...] = alpha * l_ref[...] + jnp.sum(p, axis=-1, keepdims=True)
    p = p.astype(BF16)
    w = PAGE * C_KV
    pv = _mm(p[:, 0:w], vc_refs[0][...].astype(BF16))
    for i in range(1, pp):
        pv = pv + _mm(p[:, i * w:(i + 1) * w], vc_refs[i][...].astype(BF16))
    accc_ref[...] = alpha * accc_ref[...] + pv
    m_ref[...] = m_new

    qdb = qd.astype(BF16)
    kd_all = jnp.concatenate([kd_refs[i][...].astype(BF16) for i in range(pp)], axis=1)
    z = _mm(qdb, kd_all)
    sp = _softplus(z)
    hi = sp.astype(BF16)
    lo = (sp - hi.astype(F32)).astype(BF16)
    stacked = jnp.concatenate(
        [jnp.concatenate([hi[:, i * PAGE:(i + 1) * PAGE], lo[:, i * PAGE:(i + 1) * PAGE]], axis=1)
         for i in range(pp)], axis=0)
    tl_all = _mm(stacked, uu_ref[...])
    c = c_ref[...]
    acc = accd_ref[...]
    for i in range(pp):
        tl = tl_all[i * QROWS:(i + 1) * QROWS]
        a = jnp.exp(z[:, i * PAGE:(i + 1) * PAGE] - tl - c)
        acc = acc + _nt(a.astype(BF16), vd_refs[i][...].astype(BF16))
        c = c + tl[:, 0:1]
    c_ref[...] = c
    accd_ref[...] = acc

    @pl.when(j == pl.num_programs(1) - 1)
    def _():
        lam = _lambda(lamv_ref[...], lambda_init)
        o = accc_ref[...] / l_ref[...]
        half = t_len * C_GROUP
        for hk in range(C_KV):
            o0 = o[(hk * 2) * half:(hk * 2 + 1) * half, :]
            o1 = o[(hk * 2 + 1) * half:(hk * 2 + 2) * half, :]
            oc_ref[hk * half:(hk + 1) * half, :] = _sub_rms(o0, o1, lam, g_ref[...], lambda_init)
        per = t_len * D_GROUP
        accd = accd_ref[...]
        for hk in range(D_KV):
            od_ref[hk * per:(hk + 1) * per, :] = accd[hk * per:(hk + 1) * per, hk * HD:(hk + 1) * HD]


def _sample_odd(ps, pools, page_base, page_table, lamv, subln_g, lambda_init, t_len=4):
    n, npages = page_table.shape
    pp = PAGES_PER_STEP
    ps3 = ps.reshape(n, t_len, ps.shape[1])
    eye2 = jnp.eye(2, dtype=F32)
    eye4 = jnp.eye(4, dtype=F32)
    qc = ps3[:, :, 0:1024].reshape(n, t_len, C_KV, C_GROUP, 2, HD).transpose(0, 2, 4, 1, 3, 5)
    qc = (qc[:, :, :, :, :, None, :] * eye2[None, None, :, None, None, :, None]).reshape(n, QROWS, 128)
    qd = ps3[:, :, 1536:2560].reshape(n, t_len, D_KV, D_GROUP, HD).transpose(0, 2, 1, 3, 4)
    qd = (qd[:, :, :, :, None, :] * eye4[None, :, None, None, :, None]).reshape(n, QROWS, 256)
    pt = page_table + page_base

    def page_spec(i):
        return pl.BlockSpec((None, 256, PAGE), lambda b, j, pt_ref: (pt_ref[b, npages - 1 - (j * pp + i)], 0, 0))

    per_b = lambda w, rws: pl.BlockSpec((None, rws, w), lambda b, j, pt_ref: (b, 0, 0))
    fixed = lambda shape: pl.BlockSpec(shape, lambda b, j, pt_ref: (0, 0))
    in_specs = [per_b(128, QROWS), per_b(256, QROWS), per_b(ps.shape[1], t_len)]
    ins = [qc, qd, ps3]
    for pool in pools:
        for i in range(pp):
            in_specs.append(page_spec(i))
            ins.append(pool)
    in_specs += [fixed((2 * PAGE, PAGE)), fixed((4, HD)), fixed((1, 2 * HD))]
    ins += [_tri_matrix(PAGE), lamv, subln_g.reshape(1, 2 * HD)]
    oc, od = pl.pallas_call(
        functools.partial(_sample_odd_kernel, npages=npages, t_len=t_len, past_len=npages * PAGE,
                          lambda_init=lambda_init),
        grid_spec=pltpu.PrefetchScalarGridSpec(
            num_scalar_prefetch=1,
            grid=(n, npages // pp),
            in_specs=in_specs,
            out_specs=[per_b(2 * HD, C_KV * t_len * C_GROUP), per_b(HD, D_KV * t_len * D_GROUP)],
            scratch_shapes=[pltpu.VMEM((QROWS, 1), F32), pltpu.VMEM((QROWS, 1), F32), pltpu.VMEM((QROWS, 128), F32),
                            pltpu.VMEM((QROWS, 1), F32), pltpu.VMEM((QROWS, 256), F32)]),
        out_shape=[jax.ShapeDtypeStruct((n, C_KV * t_len * C_GROUP, 2 * HD), F32),
                   jax.ShapeDtypeStruct((n, D_KV * t_len * D_GROUP, HD), F32)],
        compiler_params=_params(("parallel", "arbitrary")),
        name="sample_odd_attn",
    )(pt, *ins)
    oc = oc.reshape(n, C_KV, t_len, C_GROUP, 2 * HD).transpose(0, 2, 1, 3, 4).reshape(n * t_len, C_HEADS * 2 * HD)
    od = od.reshape(n, D_KV, t_len, D_GROUP, HD).transpose(0, 2, 1, 3, 4).reshape(n * t_len, D_HEADS * HD)
    return oc, od


def kernel(x_prompt, x_sample, cache_a1, cache_a2, cache_a3, cache_b, cache_c_k, cache_c_v, cache_d_k, cache_d_v,
           state_conv, page_table, w_in_even, w_out_even, sinks_b, w_in_odd, w_out_odd,
           lam_q1, lam_k1, lam_q2, lam_k2, subln_g, w_ffn_a, conv_w, conv_b, w_ffn_g, w_ffn_down,
           ln_mix_g, ln_mix_b, ln_ffn_g, ln_ffn_b):
    bsz, seq, d = x_prompt.shape
    nb, t_len, _ = x_sample.shape
    assert bsz == 1, "the prompt group is one sequence"
    xp = x_prompt.reshape(seq, d)
    xs = x_sample.reshape(nb * t_len, d)
    slopes = [_slope(h) for h in range(8)]
    a_off = [h * HD for h in range(A_HEADS)]
    b_off = [(h // (B_HEADS // B_KV)) * HD for h in range(B_HEADS)]
    n_phys = cache_c_k.shape[1]
    pools = [c.reshape(c.shape[0] * n_phys, PAGE * C_KV, 2 * HD) for c in (cache_c_k, cache_c_v)]
    pools += [jnp.transpose(c, (0, 1, 3, 4, 2)).reshape(c.shape[0] * n_phys, D_KV * HD, PAGE)
              for c in (cache_d_k, cache_d_v)]
    even_p, even_s, odd_p, odd_s, conv_p, conv_s = [], [], [], [], [], []
    for layer in range(DEPTH):
        i = layer // 2
        if layer % 2 == 0:
            w_in = jnp.pad(w_in_even[i].astype(BF16), ((0, 0), (0, EVEN_PAD - EVEN_IN)))
            w_out = w_out_even[i].astype(BF16)
            pp = _matmul(xp, w_in, 512, 1408)
            ps = _matmul(xs, w_in, 512, 1408)
            prev = None
            for g, (_, dil) in enumerate(A_PATTERNS):
                last = g == len(A_PATTERNS) - 1
                res = _band_prompt(pp, dil, g * 512, 1536 + g * 512, 3072 + g * 512, 512, a_off, slopes,
                                   prev=prev, want_lse=not last)
                prev = None if last else res
            oa_p = res
            ob_p = _band_prompt(pp, 1, 4608, 5120, 5248, 128, b_off, slopes, sinks=sinks_b[i].reshape(B_HEADS))
            oa_s, ob_s = _sample_even(ps, cache_a1, cache_a2, cache_a3, cache_b, i, sinks_b[i].reshape(B_HEADS), t_len)
            w1, w2 = w_out[:512], w_out[512:]
            rows_p, rows_s = [], []
            for g, (win, _) in enumerate(A_PATTERNS):
                w = min(win, seq)
                kv = jnp.stack([pp[seq - w:, 1536 + g * 512:2048 + g * 512].reshape(w, A_HEADS, HD),
                                pp[seq - w:, 3072 + g * 512:3584 + g * 512].reshape(w, A_HEADS, HD)], axis=1)
                rows_p.append(kv[None])
                rows_s.append(jnp.stack([ps[:, 1536 + g * 512:2048 + g * 512].reshape(nb, t_len, A_HEADS, HD),
                                         ps[:, 3072 + g * 512:3584 + g * 512].reshape(nb, t_len, A_HEADS, HD)], axis=2))
            w = min(128, seq)
            rows_p.append(jnp.stack([pp[seq - w:, 5120:5248].reshape(w, B_KV, HD),
                                     pp[seq - w:, 5248:5376].reshape(w, B_KV, HD)], axis=1)[None])
            rows_s.append(jnp.stack([ps[:, 5120:5248].reshape(nb, t_len, B_KV, HD),
                                     ps[:, 5248:5376].reshape(nb, t_len, B_KV, HD)], axis=2))
            even_p.append(rows_p)
            even_s.append(rows_s)
            mix_p, mix_s = (oa_p, ob_p), (oa_s, ob_s)
        else:
            lambda_init = 0.8 - 0.6 * math.exp(-0.3 * layer)
            lamv = jnp.stack([lam_q1[i], lam_k1[i], lam_q2[i], lam_k2[i]])
            w_in = w_in_odd[i].astype(BF16)
            w_out = w_out_odd[i].astype(BF16)
            pp = _matmul(xp, w_in, 512, 1024)
            ps = _matmul(xs, w_in, 512, 1024)
            wvt = jnp.concatenate([w_in[:, 1280:1536], w_in[:, 2816:3072]], axis=1).T
            vt = _matmul_t(wvt, xp)
            oc_p = _diff_prompt(pp, vt, lamv, subln_g[i], lambda_init)
            od_p = _sb_prompt(pp, vt)
            oc_s, od_s = _sample_odd(ps, pools, i * n_phys, page_table, lamv, subln_g[i], lambda_init, t_len)
            w1, w2 = w_out[:1024], w_out[1024:]
            odd_p.append((pp[:, 1024:1280].reshape(1, seq, C_KV, 2 * HD), pp[:, 1280:1536].reshape(1, seq, C_KV, 2 * HD),
                          pp[:, 2560:2816].reshape(1, seq, D_KV, HD), pp[:, 2816:3072].reshape(1, seq, D_KV, HD)))
            odd_s.append((ps[:, 1024:1280].reshape(nb, t_len, C_KV, 2 * HD), ps[:, 1280:1536].reshape(nb, t_len, C_KV, 2 * HD),
                          ps[:, 2560:2816].reshape(nb, t_len, D_KV, HD), ps[:, 2816:3072].reshape(nb, t_len, D_KV, HD)))
            mix_p, mix_s = (oc_p, od_p), (oc_s, od_s)
        xp = _outproj_ln(mix_p[0], mix_p[1], w1, w2, xp, ln_mix_g[layer], ln_mix_b[layer])
        xs = _outproj_ln(mix_s[0], mix_s[1], w1, w2, xs, ln_mix_g[layer], ln_mix_b[layer])
        wa, wg, wd = w_ffn_a[layer].astype(BF16), w_ffn_g[layer].astype(BF16), w_ffn_down[layer].astype(BF16)
        ffn_args = (wa, wg, wd, conv_w[layer], conv_b[layer], ln_ffn_g[layer], ln_ffn_b[layer])
        xp, tail_p = _ffn(xp, *ffn_args)
        st = state_conv[layer]
        p1 = jnp.concatenate([st[:, 1:2], jnp.zeros((nb, t_len - 1, D_FF), F32)], axis=1).reshape(nb * t_len, D_FF)
        p2 = jnp.concatenate([st, jnp.zeros((nb, t_len - 2, D_FF), F32)], axis=1).reshape(nb * t_len, D_FF)
        xs, tail_s = _ffn(xs, *ffn_args, prev=(p1, p2), t_len=t_len)
        conv_p.append(tail_p[-2:][None])
        conv_s.append(tail_s.reshape(nb, t_len, D_FF)[:, t_len - 2:])
    stack = lambda per_layer, j: jnp.stack([entry[j] for entry in per_layer])
    outs = [xp.reshape(1, seq, d), xs.reshape(nb, t_len, d)]
    for j in range(4):
        outs += [stack(even_p, j), stack(even_s, j)]
    for j in range(4):
        outs += [stack(odd_p, j), stack(odd_s, j)]
    outs += [jnp.stack(conv_p), jnp.stack(conv_s)]
    return tuple(outs)
```

```python
import functools
import math

import numpy as np
import jax
import jax.numpy as jnp
from jax import lax
from jax.experimental import pallas as pl
from jax.experimental.pallas import tpu as pltpu

F32 = jnp.float32
BF16 = jnp.bfloat16

D_MODEL = 2048
DEPTH = 2
PAGE = 128
HD = 64
BLK = 128
A_PATTERNS = ((128, 1), (512, 4), (2048, 16))
A_HEADS = 8
B_HEADS = 8
B_KV = 2
C_HEADS = 8
C_KV = 2
D_HEADS = 16
D_KV = 4
D_FF = 5632
LN_EPS = 1e-5
NEG = -1e30
ALPHA = (2 * DEPTH) ** 0.25
EVEN_IN = 5376
EVEN_PAD = 5632
ODD_IN = 3072
SCALE = HD ** -0.5

VMEM_LIMIT = 56 * 1024 * 1024


def _params(sem):
    return pltpu.CompilerParams(dimension_semantics=sem, vmem_limit_bytes=VMEM_LIMIT)


def _nt(a, b):
    return lax.dot_general(a, b, (((1,), (1,)), ((), ())), preferred_element_type=F32)


def _mm(a, b):
    return jnp.dot(a, b, preferred_element_type=F32)


def _slope(h, n=8):
    return 2.0 ** (-8.0 * (h + 1) / n)


def _matmul_kernel(x_ref, w_ref, o_ref, xb_ref):
    @pl.when(pl.program_id(1) == 0)
    def _():
        xb_ref[...] = x_ref[...].astype(BF16)

    o_ref[...] = _mm(xb_ref[...], w_ref[...])


def _matmul(x, w, tm, tn):
    m, k = x.shape
    n = w.shape[1]
    tm = min(tm, m)
    return pl.pallas_call(
        _matmul_kernel,
        grid=(m // tm, n // tn),
        in_specs=[pl.BlockSpec((tm, k), lambda i, j: (i, 0)),
                  pl.BlockSpec((k, tn), lambda i, j: (0, j))],
        out_specs=pl.BlockSpec((tm, tn), lambda i, j: (i, j)),
        out_shape=jax.ShapeDtypeStruct((m, n), F32),
        scratch_shapes=[pltpu.VMEM((tm, k), BF16)],
        compiler_params=_params(("parallel", "arbitrary")),
        name="proj_matmul",
    )(x, w)


def _layer_norm_rows(r, g, b):
    mu = jnp.mean(r, axis=-1, keepdims=True)
    d = r - mu
    var = jnp.mean(d * d, axis=-1, keepdims=True)
    return d * lax.rsqrt(var + LN_EPS) * g + b


def _outproj_ln_kernel(m1_ref, m2_ref, w1_ref, w2_ref, x_ref, g_ref, b_ref, o_ref):
    y = _mm(m1_ref[...].astype(BF16), w1_ref[...]) + _mm(m2_ref[...].astype(BF16), w2_ref[...])
    o_ref[...] = _layer_norm_rows(ALPHA * x_ref[...] + y, g_ref[...], b_ref[...])


def _outproj_ln(m1, m2, w1, w2, x, g, b, tm=256):
    m, d = x.shape
    k1, k2 = m1.shape[1], m2.shape[1]
    tm = min(tm, m)
    row = lambda i: (i, 0)
    fixed = lambda i: (0, 0)
    return pl.pallas_call(
        _outproj_ln_kernel,
        grid=(m // tm,),
        in_specs=[pl.BlockSpec((tm, k1), row), pl.BlockSpec((tm, k2), row),
                  pl.BlockSpec((k1, d), fixed), pl.BlockSpec((k2, d), fixed),
                  pl.BlockSpec((tm, d), row), pl.BlockSpec((1, d), fixed), pl.BlockSpec((1, d), fixed)],
        out_specs=pl.BlockSpec((tm, d), row),
        out_shape=jax.ShapeDtypeStruct((m, d), F32),
        compiler_params=_params(("parallel",)),
        name="outproj_ln",
    )(m1, m2, w1, w2, x, g.reshape(1, d), b.reshape(1, d))


FFN_HALO = 16


def _ffn_kernel(*refs, tm, sample, t_len):
    if sample:
        x_ref, p1_ref, p2_ref, wa_ref, wg_ref, wd_ref, cw_ref, cb_ref, g_ref, b_ref, o_ref, tail_ref, xb_ref, acc_ref = refs
    else:
        x_ref, xh_ref, wa_ref, wg_ref, wd_ref, cw_ref, cb_ref, g_ref, b_ref, o_ref, tail_ref, xb_ref, acc_ref = refs
    i = pl.program_id(0)
    f = pl.program_id(1)

    @pl.when(f == 0)
    def _():
        xb_ref[FFN_HALO:, :] = x_ref[...].astype(BF16)
        if sample:
            xb_ref[:FFN_HALO, :] = jnp.zeros((FFN_HALO, x_ref.shape[1]), BF16)
        else:
            halo = jnp.where(i > 0, xh_ref[...], 0.0)
            xb_ref[:FFN_HALO, :] = halo.astype(BF16)
        acc_ref[...] = jnp.zeros_like(acc_ref)

    a_ext = _mm(xb_ref[...], wa_ref[...])
    gate = _mm(xb_ref[FFN_HALO:, :], wg_ref[...])
    a = a_ext[FFN_HALO:, :]
    a1 = pltpu.roll(a_ext, 1, 0)[FFN_HALO:, :]
    a2 = pltpu.roll(a_ext, 2, 0)[FFN_HALO:, :]
    if sample:
        t = lax.broadcasted_iota(jnp.int32, a.shape, 0) % t_len
        a1 = jnp.where(t == 0, p1_ref[...], a1)
        a2 = jnp.where(t < 2, p2_ref[...], a2)
    cw = cw_ref[...]
    c = a2 * cw[0:1, :] + a1 * cw[1:2, :] + a * cw[2:3, :] + cb_ref[...]
    h = (c * jax.nn.sigmoid(c)) * gate
    acc_ref[...] += _mm(h.astype(BF16), wd_ref[...])
    tail_ref[...] = a[tm - tail_ref.shape[0]:, :]

    @pl.when(f == pl.num_programs(1) - 1)
    def _():
        o_ref[...] = _layer_norm_rows(ALPHA * x_ref[...] + acc_ref[...], g_ref[...], b_ref[...])


def _ffn(x, wa, wg, wd, cw, cb, g, b, prev=None, t_len=4, tm=512, tf=512):
    m, d = x.shape
    dff = wa.shape[1]
    tm = min(tm, m)
    sample = prev is not None
    row = lambda i, f: (i, 0)
    col = lambda i, f: (0, f)
    wcol = pl.BlockSpec((d, tf), col)
    ins = [x]
    specs = [pl.BlockSpec((tm, d), row)]
    if sample:
        ins += list(prev)
        specs += [pl.BlockSpec((tm, tf), lambda i, f: (i, f))] * 2
        tail_rows = tm
        tail_spec = pl.BlockSpec((tm, tf), lambda i, f: (i, f))
        tail_shape = (m, dff)
    else:
        ins += [x]
        hb = tm // FFN_HALO
        specs += [pl.BlockSpec((FFN_HALO, d), lambda i, f: (jnp.maximum(i * hb - 1, 0), 0))]
        tail_rows = 8
        tail_spec = pl.BlockSpec((8, tf), lambda i, f: (i, f))
        tail_shape = (8 * (m // tm), dff)
    ins += [wa, wg, wd, cw, cb.reshape(1, dff), g.reshape(1, d), b.reshape(1, d)]
    specs += [wcol, wcol, pl.BlockSpec((tf, d), lambda i, f: (f, 0)),
              pl.BlockSpec((3, tf), col), pl.BlockSpec((1, tf), col),
              pl.BlockSpec((1, d), lambda i, f: (0, 0)), pl.BlockSpec((1, d), lambda i, f: (0, 0))]
    return pl.pallas_call(
        functools.partial(_ffn_kernel, tm=tm, sample=sample, t_len=t_len),
        grid=(m // tm, dff // tf),
        in_specs=specs,
        out_specs=[pl.BlockSpec((tm, d), row), tail_spec],
        out_shape=[jax.ShapeDtypeStruct((m, d), F32), jax.ShapeDtypeStruct(tail_shape, F32)],
        scratch_shapes=[pltpu.VMEM((tm + FFN_HALO, d), BF16), pltpu.VMEM((tm, d), F32)],
        compiler_params=_params(("arbitrary", "arbitrary")),
        name="conv_ffn_sample" if sample else "conv_ffn_prompt",
    )(*ins)


def _band_prompt_kernel(*refs, dil, sub, kv_off, head_step, has_sink, has_prev, want_lse):
    it = iter(refs)
    q_ref, kc_ref, kp_ref, vc_ref, vp_ref = (next(it) for _ in range(5))
    sink_ref = next(it) if has_sink else None
    op_ref, lp_ref = (next(it), next(it)) if has_prev else (None, None)
    o_ref = next(it)
    l_ref = next(it) if want_lse else None
    c = pl.program_id(0)
    hg = pl.program_id(1)
    nh = len(kv_off)
    group_scale = jnp.float32(1.0)
    for n in range(1, 4):
        group_scale = jnp.where(hg == n, jnp.float32(head_step ** n), group_scale)
    qi = lax.broadcasted_iota(jnp.int32, (BLK, BLK), 0)
    kj = lax.broadcasted_iota(jnp.int32, (BLK, BLK), 1)
    mask_c = kj <= qi
    mask_p = kj >= qi
    dist_c = (qi - kj).astype(F32) * float(dil)
    dist_p = (BLK + qi - kj).astype(F32) * float(dil)

    def rows(start):
        return pl.ds(start, BLK, stride=dil) if dil > 1 else pl.ds(start, BLK)

    for res in range(dil):
        for w in range(sub):
            cur = rows(res + BLK * w * dil)
            q_all = q_ref[cur, :]
            kc_all, vc_all = kc_ref[cur, :], vc_ref[cur, :]
            if w == 0:
                kp_all, vp_all = kp_ref[rows(res), :], vp_ref[rows(res), :]
                mp = jnp.logical_and(mask_p, c > 0)
            else:
                prv = rows(res + BLK * (w - 1) * dil)
                kp_all, vp_all = kc_ref[prv, :], vc_ref[prv, :]
                mp = mask_p
            if has_prev:
                op_all, lp_all = op_ref[cur, :], lp_ref[cur, :]
            o_parts, l_parts = [], []
            for h in range(nh):
                hs = slice(h * HD, (h + 1) * HD)
                ks = slice(kv_off[h], kv_off[h] + HD)
                slope = _slope(h) * group_scale
                q = q_all[:, hs].astype(BF16)
                s_c = _nt(q, kc_all[:, ks].astype(BF16)) * SCALE - slope * dist_c
                s_p = _nt(q, kp_all[:, ks].astype(BF16)) * SCALE - slope * dist_p
                s_c = jnp.where(mask_c, s_c, NEG)
                s_p = jnp.where(mp, s_p, NEG)
                m = jnp.maximum(jnp.max(s_c, axis=-1, keepdims=True), jnp.max(s_p, axis=-1, keepdims=True))
                p_c = jnp.exp(s_c - m)
                p_p = jnp.exp(s_p - m)
                l = jnp.sum(p_c, axis=-1, keepdims=True) + jnp.sum(p_p, axis=-1, keepdims=True)
                if has_sink:
                    l = l + jnp.exp(sink_ref[h] - m)
                o = (_mm(p_c.astype(BF16), vc_all[:, ks].astype(BF16))
                     + _mm(p_p.astype(BF16), vp_all[:, ks].astype(BF16)))
                o = o / l
                lse = m + jnp.log(l)
                if has_prev:
                    lse_prev = lp_all[:, hs]
                    lse_new = jnp.logaddexp(lse_prev, lse)
                    o = op_all[:, hs] * jnp.exp(lse_prev - lse_new) + o * jnp.exp(lse - lse_new)
                    lse = lse_new
                o_parts.append(o)
                l_parts.append(jnp.broadcast_to(lse, (BLK, HD)))
            o_ref[cur, :] = jnp.concatenate(o_parts, axis=1)
            if want_lse:
                l_ref[cur, :] = jnp.concatenate(l_parts, axis=1)


def _band_prompt(proj, dil, sub, q_col, k_col, v_col, nh, kv_w, kv_off, head_step=1.0, sinks=None, prev=None,
                 want_lse=False):
    s, w = proj.shape
    chunk = BLK * dil * sub
    tail = BLK * dil
    qw = nh * HD
    ngroups = 512 // qw
    kv_groups = 1 if kv_w == 128 and nh == 8 else ngroups
    kcol = (lambda g: k_col // kv_w + g) if kv_groups > 1 else (lambda g: k_col // kv_w)
    vcol = (lambda g: v_col // kv_w + g) if kv_groups > 1 else (lambda g: v_col // kv_w)
    qspec = pl.BlockSpec((chunk, qw), lambda c, g: (c, q_col // qw + g))
    kc = pl.BlockSpec((chunk, kv_w), lambda c, g: (c, kcol(g)))
    kp = pl.BlockSpec((tail, kv_w), lambda c, g: (jnp.maximum(c * sub - 1, 0), kcol(g)))
    vc = pl.BlockSpec((chunk, kv_w), lambda c, g: (c, vcol(g)))
    vp = pl.BlockSpec((tail, kv_w), lambda c, g: (jnp.maximum(c * sub - 1, 0), vcol(g)))
    ospec = pl.BlockSpec((chunk, qw), lambda c, g: (c, g))
    ins = [proj, proj, proj, proj, proj]
    specs = [qspec, kc, kp, vc, vp]
    if sinks is not None:
        ins.append(sinks)
        specs.append(pl.BlockSpec(memory_space=pltpu.SMEM))
    if prev is not None:
        ins += [prev[0], prev[1]]
        specs += [ospec, ospec]
    oshape = jax.ShapeDtypeStruct((s, 512), F32)
    return pl.pallas_call(
        functools.partial(_band_prompt_kernel, dil=dil, sub=sub, kv_off=tuple(kv_off), head_step=head_step,
                          has_sink=sinks is not None, has_prev=prev is not None, want_lse=want_lse),
        grid=(s // chunk, ngroups),
        in_specs=specs,
        out_specs=[ospec, ospec] if want_lse else ospec,
        out_shape=[oshape, oshape] if want_lse else oshape,
        compiler_params=_params(("parallel", "parallel")),
        name=f"band_prompt_d{dil}",
    )(*ins)


def _col_consts(values, rows):
    r = lax.broadcasted_iota(jnp.int32, (rows, 1), 0)
    col = jnp.zeros((rows, 1), F32)
    for h, v in enumerate(values):
        col = jnp.where(r == h, v, col)
    return col


def _sample_even_kernel(ps_ref, a1_ref, a2_ref, a3_ref, cb_ref, fold_ref, sink_ref, oa_ref, ob_ref, *, t_len):
    nh = 8
    nr = t_len * nh
    row = lax.broadcasted_iota(jnp.int32, (nr, 1), 0)
    t_col = row // nh
    h_col = row % nh
    hm = (lax.broadcasted_iota(jnp.int32, (nr, nh * HD), 1) // HD == h_col).astype(F32)
    slope = jnp.zeros((nr, 1), F32)
    sink_col = jnp.zeros((nr, 1), F32)
    for h in range(nh):
        slope = jnp.where(h_col == h, _slope(h), slope)
        sink_col = jnp.where(h_col == h, sink_ref[h], sink_col)

    def q_rows(col):
        return jnp.concatenate([jnp.broadcast_to(ps_ref[t:t + 1, col:col + 512], (nh, 512))
                                for t in range(t_len)], axis=0) * hm

    def attend(qk, q_new, kt_ref_slice, vt_ref_slice, k_col, v_col, width, dil, sinks):
        kt = kt_ref_slice.astype(BF16)
        length = kt.shape[1]
        dd = length + t_col - lax.broadcasted_iota(jnp.int32, (nr, length), 1)
        valid = jnp.logical_and((dd & (dil - 1)) == 0, dd <= dil * BLK)
        s = jnp.where(valid, _mm(qk, kt) * SCALE - slope * dd.astype(F32), NEG)
        m = jnp.max(s, axis=-1, keepdims=True)
        new = []
        for tp in range(t_len):
            dn = t_col - tp
            ok = jnp.logical_and(dn >= 0, (dn & (dil - 1)) == 0)
            sc = jnp.sum(q_new * ps_ref[tp:tp + 1, k_col:k_col + width], axis=-1, keepdims=True) * SCALE
            sc = jnp.where(ok, sc - slope * dn.astype(F32), NEG)
            new.append(sc)
            m = jnp.maximum(m, sc)
        p = jnp.exp(s - m)
        l = jnp.sum(p, axis=-1, keepdims=True)
        o = _nt(p.astype(BF16), vt_ref_slice.astype(BF16))
        for tp in range(t_len):
            pn = jnp.exp(new[tp] - m)
            l = l + pn
            o = o + pn * ps_ref[tp:tp + 1, v_col:v_col + width]
        if sinks:
            l = l + jnp.exp(sink_col - m)
        return o / l, m + jnp.log(l)

    def head_rows(o_wide):
        return jnp.sum((o_wide * hm).reshape(t_len, nh, nh * HD), axis=1)

    outs, lses = [], []
    for g, (buf, (_, dil)) in enumerate(zip((a1_ref, a2_ref, a3_ref), A_PATTERNS)):
        q = q_rows(g * 512)
        o, lse = attend(q.astype(BF16), q, buf[0:512, :], buf[512:1024, :],
                        1536 + g * 512, 3072 + g * 512, 512, dil, False)
        outs.append(o)
        lses.append(lse)
    mx = jnp.maximum(jnp.maximum(lses[0], lses[1]), lses[2])
    es = [jnp.exp(l_ - mx) for l_ in lses]
    merged = (es[0] * outs[0] + es[1] * outs[1] + es[2] * outs[2]) / (es[0] + es[1] + es[2])
    oa_ref[...] = head_rows(merged)

    qf = _mm(q_rows(4608).astype(BF16), fold_ref[...])
    o, _ = attend(qf.astype(BF16), qf, cb_ref[0:128, :], cb_ref[128:256, :], 5120, 5248, 128, 1, True)
    o_sw = pltpu.roll(o, HD, 1)
    o_sel = jnp.where((h_col // 4) == (h_col % 2), o, o_sw)
    ob_ref[...] = head_rows(jnp.concatenate([o_sel] * 4, axis=1))


def _fold_matrix():
    c = np.arange(512)[:, None]
    l = np.arange(128)[None, :]
    return jnp.asarray(((c % HD == l % HD) & (l // HD == (c // HD) // 4)).astype(np.float32), BF16)


def _sample_even(ps, a1, a2, a3, cb, layer, sinks, t_len=4):
    nl, n = a1.shape[0], a1.shape[1]
    ps3 = ps.reshape(n, t_len, ps.shape[1])

    def rows_last(c):
        return jnp.transpose(c, (0, 1, 3, 4, 5, 2)).reshape(nl * n, -1, c.shape[2])

    a1v, a2v, a3v, cbv = rows_last(a1), rows_last(a2), rows_last(a3), rows_last(cb)
    b3 = lambda arr: pl.BlockSpec((None,) + arr.shape[1:], lambda i: (layer * n + i, 0, 0))
    ospec = pl.BlockSpec((None, t_len, 512), lambda i: (i, 0, 0))
    oa, ob = pl.pallas_call(
        functools.partial(_sample_even_kernel, t_len=t_len),
        grid=(n,),
        in_specs=[pl.BlockSpec((None, t_len, ps.shape[1]), lambda i: (i, 0, 0)),
                  b3(a1v), b3(a2v), b3(a3v), b3(cbv),
                  pl.BlockSpec((512, 128), lambda i: (0, 0)),
                  pl.BlockSpec(memory_space=pltpu.SMEM)],
        out_specs=[ospec, ospec],
        out_shape=[jax.ShapeDtypeStruct((n, t_len, 512), F32)] * 2,
        compiler_params=_params(("parallel",)),
        name="sample_even_attn",
    )(ps3, a1v, a2v, a3v, cbv, _fold_matrix(), sinks)
    return oa.reshape(n * t_len, 512), ob.reshape(n * t_len, 512)


C_GROUP = C_HEADS // C_KV
D_GROUP = D_HEADS // D_KV


def _lambda(lamv, lambda_init):
    s1 = jnp.sum(lamv[0:1, :] * lamv[1:2, :], axis=-1, keepdims=True)
    s2 = jnp.sum(lamv[2:3, :] * lamv[3:4, :], axis=-1, keepdims=True)
    return jnp.exp(s1) - jnp.exp(s2) + lambda_init


def _sub_rms(o0, o1, lam, g, lambda_init):
    d = o0 - lam * o1
    return d * lax.rsqrt(jnp.mean(d * d, axis=-1, keepdims=True) + LN_EPS) * g * (1.0 - lambda_init)


def _softplus(z):
    return jnp.maximum(z, 0.0) + jnp.log(1.0 + jnp.exp(-jnp.abs(z)))


def _tri_matrix(n):
    u = (np.arange(n)[:, None] >= np.arange(n)[None, :]).astype(np.float32)
    return jnp.asarray(np.concatenate([u, u], axis=0), BF16)


def _rev_cumsum(sp, uu):
    hi = sp.astype(BF16)
    lo = (sp - hi.astype(F32)).astype(BF16)
    return _mm(jnp.concatenate([hi, lo], axis=1), uu)


def _matmul_t_kernel(wt_ref, x_ref, o_ref):
    o_ref[...] = _nt(wt_ref[...], x_ref[...].astype(BF16))


def _matmul_t(wt, x, tm=512):
    n, k = wt.shape
    m = x.shape[0]
    tm = min(tm, m)
    return pl.pallas_call(
        _matmul_t_kernel,
        grid=(m // tm,),
        in_specs=[pl.BlockSpec((n, k), lambda i: (0, 0)), pl.BlockSpec((tm, k), lambda i: (i, 0))],
        out_specs=pl.BlockSpec((n, tm), lambda i: (0, i)),
        out_shape=jax.ShapeDtypeStruct((n, m), F32),
        compiler_params=_params(("parallel",)),
        name="proj_matmul_t",
    )(wt, x)


def _diff_prompt_kernel(q_ref, k_ref, vt_ref, lamv_ref, g_ref, o_ref, b0_ref, m_ref, l_ref, acc_ref, *, tq, lambda_init):
    hk = pl.program_id(0)
    i = pl.program_id(1)
    rows = C_GROUP * tq
    lane = lax.broadcasted_iota(jnp.int32, (1, rows), 1)
    slope = jnp.zeros((1, rows), F32)
    for g in range(C_GROUP):
        slope = jnp.where(lane // tq == g, _slope(g), slope)
    slope = slope * jnp.where(hk == 0, 1.0, 2.0 ** (-C_GROUP)).astype(F32)
    kl = lax.broadcasted_iota(jnp.int32, (tq, rows), 0)
    ql = lax.broadcasted_iota(jnp.int32, (tq, rows), 1) % tq
    b0_ref[...] = slope * (kl - ql).astype(F32)
    qs = []
    for m in range(2):
        parts = [q_ref[:, (g * 2 + m) * HD:(g * 2 + m + 1) * HD] for g in range(C_GROUP)]
        qs.append((jnp.concatenate(parts, axis=0) * SCALE).astype(BF16))
    m_ref[...] = jnp.full(m_ref.shape, NEG, F32)
    l_ref[...] = jnp.zeros_like(l_ref)
    acc_ref[...] = jnp.zeros_like(acc_ref)

    def block(j, diagonal):
        start = pl.multiple_of(j * tq, tq)
        k = k_ref[pl.ds(start, tq), :].astype(BF16)
        vt = vt_ref[:, pl.ds(start, tq)].astype(BF16)
        cj = -slope * ((i - j) * tq).astype(F32)
        for m in range(2):
            s = _nt(k[:, m * HD:(m + 1) * HD], qs[m]) + b0_ref[...]
            if diagonal:
                s = jnp.where(kl <= ql, s, NEG)
            m_old = m_ref[m]
            m_new = jnp.maximum(m_old, jnp.max(s, axis=0, keepdims=True) + cj)
            alpha = jnp.exp(m_old - m_new)
            p = jnp.exp(s + (cj - m_new))
            l_ref[m] = alpha * l_ref[m] + jnp.sum(p, axis=0, keepdims=True)
            acc_ref[m] = alpha * acc_ref[m] + _mm(vt, p.astype(BF16))
            m_ref[m] = m_new

    def off_diagonal(j, carry):
        block(j, False)
        return carry

    lax.fori_loop(0, i, off_diagonal, 0)
    block(i, True)
    lam = _lambda(lamv_ref[...], lambda_init)
    d = acc_ref[0] / l_ref[0] - lam * (acc_ref[1] / l_ref[1])
    d = d * lax.rsqrt(jnp.mean(d * d, axis=0, keepdims=True) + LN_EPS) * g_ref[...] * (1.0 - lambda_init)
    for g in range(C_GROUP):
        o_ref[:, g * 2 * HD:(g + 1) * 2 * HD] = d[:, g * tq:(g + 1) * tq].T


def _diff_prompt(proj, vt, lamv, subln_g, lambda_init, tq=256):
    s = proj.shape[0]
    rows = C_GROUP * tq
    return pl.pallas_call(
        functools.partial(_diff_prompt_kernel, tq=tq, lambda_init=lambda_init),
        grid=(C_KV, s // tq),
        in_specs=[pl.BlockSpec((tq, 512), lambda h, i: (i, h)),
                  pl.BlockSpec((s, 128), lambda h, i: (0, 8 + h)),
                  pl.BlockSpec((2 * HD, s), lambda h, i: (h, 0)),
                  pl.BlockSpec((4, HD), lambda h, i: (0, 0)),
                  pl.BlockSpec((2 * HD, 1), lambda h, i: (0, 0))],
        out_specs=pl.BlockSpec((tq, 512), lambda h, i: (i, h)),
        out_shape=jax.ShapeDtypeStruct((s, C_HEADS * 2 * HD), F32),
        scratch_shapes=[pltpu.VMEM((tq, rows), F32), pltpu.VMEM((2, 1, rows), F32), pltpu.VMEM((2, 1, rows), F32),
                        pltpu.VMEM((2, 2 * HD, rows), F32)],
        compiler_params=_params(("parallel", "arbitrary")),
        name="diff_attn_prompt",
    )(proj, proj, vt, lamv, subln_g.reshape(2 * HD, 1))


def _tri_matrix_t(n):
    u = (np.arange(n)[None, :] >= np.arange(n)[:, None]).astype(np.float32)
    return jnp.asarray(np.concatenate([u, u], axis=1), BF16)


def _sb_prompt_kernel(q_ref, k_ref, vt_ref, ut_ref, o_ref, c_ref, acc_ref, *, tq):
    i = pl.program_id(1)
    rows = D_GROUP * tq
    kl = lax.broadcasted_iota(jnp.int32, (tq, rows), 0)
    ql = lax.broadcasted_iota(jnp.int32, (tq, rows), 1) % tq
    qs = []
    for hl in range(2):
        parts = [q_ref[:, (hl * D_GROUP + g) * HD:(hl * D_GROUP + g + 1) * HD] for g in range(D_GROUP)]
        qs.append((jnp.concatenate(parts, axis=0) * SCALE).astype(BF16))
    c_ref[...] = jnp.zeros_like(c_ref)
    acc_ref[...] = jnp.zeros_like(acc_ref)

    def block(j, diagonal):
        start = pl.multiple_of(j * tq, tq)
        for hl in range(2):
            k = k_ref[pl.ds(start, tq), hl * HD:(hl + 1) * HD].astype(BF16)
            vt = vt_ref[hl * HD:(hl + 1) * HD, pl.ds(start, tq)].astype(BF16)
            z = _nt(k, qs[hl])
            sp = _softplus(z)
            if diagonal:
                sp = jnp.where(kl < ql, sp, 0.0)
            hi = sp.astype(BF16)
            lo = (sp - hi.astype(F32)).astype(BF16)
            tl = _mm(ut_ref[...], jnp.concatenate([hi, lo], axis=0))
            c = c_ref[hl]
            e = z - tl - c
            if diagonal:
                e = jnp.where(kl < ql, e, NEG)
            acc_ref[hl * HD:(hl + 1) * HD, :] += _mm(vt, jnp.exp(e).astype(BF16))
            c_ref[hl] = c + tl[0:1, :]

    block(i, True)

    def older(it, carry):
        block(i - 1 - it, False)
        return carry

    lax.fori_loop(0, i, older, 0)
    for g in range(D_GROUP):
        t = acc_ref[:, g * tq:(g + 1) * tq].T
        for hl in range(2):
            o_ref[:, (hl * D_GROUP + g) * HD:(hl * D_GROUP + g + 1) * HD] = t[:, hl * HD:(hl + 1) * HD]


def _sb_prompt(proj, vt, tq=256):
    s = proj.shape[0]
    rows = D_GROUP * tq
    return pl.pallas_call(
        functools.partial(_sb_prompt_kernel, tq=tq),
        grid=(D_KV // 2, s // tq),
        in_specs=[pl.BlockSpec((tq, 512), lambda h, i: (i, 3 + h)),
                  pl.BlockSpec((s, 128), lambda h, i: (0, 20 + h)),
                  pl.BlockSpec((2 * HD, s), lambda h, i: (2 + h, 0)),
                  pl.BlockSpec((tq, 2 * tq), lambda h, i: (0, 0))],
        out_specs=pl.BlockSpec((tq, 512), lambda h, i: (i, h)),
        out_shape=jax.ShapeDtypeStruct((s, D_HEADS * HD), F32),
        scratch_shapes=[pltpu.VMEM((2, 1, rows), F32), pltpu.VMEM((2 * HD, rows), F32)],
        compiler_params=_params(("parallel", "arbitrary")),
        name="sb_attn_prompt",
    )(proj, proj, vt, _tri_matrix_t(tq))


PAGES_PER_STEP = 8
QROWS = 64


def _sample_odd_kernel(pt_ref, qc_ref, qd_ref, ps_ref, *refs, npages, t_len, past_len, lambda_init):
    pp = PAGES_PER_STEP
    hrows = QROWS // C_KV
    pools = refs[0:4]
    uu_ref, lamv_ref, g_ref, oc_ref, od_ref = refs[4:9]
    bufs = refs[9:13]
    sem, m_ref, l_ref, accc_ref, c_ref, accd_ref = refs[13:]
    b = pl.program_id(0)
    j = pl.program_id(1)
    nsteps = pl.num_programs(1)
    slot = j % 2

    def page_copies(bb, jj, sl):
        out = []
        for i in range(pp):
            page = pt_ref[bb, npages - 1 - (jj * pp + i)]
            for pool, buf in zip(pools, bufs):
                out.append(pltpu.make_async_copy(pool.at[page], buf.at[sl, i], sem.at[sl]))
        return out

    @pl.when(jnp.logical_and(b == 0, j == 0))
    def _():
        for cp in page_copies(0, 0, 0):
            cp.start()

    last = jnp.logical_and(b == pl.num_programs(0) - 1, j == nsteps - 1)

    @pl.when(jnp.logical_not(last))
    def _():
        nj = jnp.where(j == nsteps - 1, 0, j + 1)
        nb = jnp.where(j == nsteps - 1, b + 1, b)
        for cp in page_copies(nb, nj, 1 - slot):
            cp.start()

    for cp in page_copies(b, j, slot):
        cp.wait()
    kc_refs = [bufs[0].at[slot, i] for i in range(pp)]
    vc_refs = [bufs[1].at[slot, i] for i in range(pp)]
    kd_refs = [bufs[2].at[slot, i] for i in range(pp)]
    vd_refs = [bufs[3].at[slot, i] for i in range(pp)]
    r = lax.broadcasted_iota(jnp.int32, (QROWS, 1), 0)
    tc_col = (r // C_GROUP) % t_len
    td_col = (r // D_GROUP) % t_len
    head_c = (r // (2 * t_len * C_GROUP)) * C_GROUP + r % C_GROUP
    slope = jnp.zeros((QROWS, 1), F32)
    for h in range(C_HEADS):
        slope = jnp.where(head_c == h, _slope(h), slope)
    qc = qc_ref[...] * SCALE
    qd = qd_ref[...] * SCALE

    @pl.when(j == 0)
    def _():
        def per_head(tp, cols):
            return jnp.concatenate(
                [jnp.broadcast_to(ps_ref[tp:tp + 1, cols + hk * 128:cols + (hk + 1) * 128], (hrows, 128))
                 for hk in range(C_KV)], axis=0)

        scs = []
        for tp in range(t_len):
            sc = jnp.sum(qc * per_head(tp, 1024), axis=-1, keepdims=True)
            sc = sc - slope * (tc_col - tp).astype(F32)
            scs.append(jnp.where(tp <= tc_col, sc, NEG))
        m = scs[0]
        for sc in scs[1:]:
            m = jnp.maximum(m, sc)
        l = jnp.zeros((QROWS, 1), F32)
        acc = jnp.zeros((QROWS, 128), F32)
        for tp in range(t_len):
            p = jnp.exp(scs[tp] - m)
            l = l + p
            acc = acc + p * per_head(tp, 1280)
        m_ref[...] = m
        l_ref[...] = l
        accc_ref[...] = acc
        c = jnp.zeros((QROWS, 1), F32)
        acc = jnp.zeros((QROWS, 256), F32)
        for tp in range(t_len - 1, -1, -1):
            z = jnp.sum(qd * ps_ref[tp:tp + 1, 2560:2816], axis=-1, keepdims=True)
            earlier = tp < td_col
            sp = jnp.where(earlier, _softplus(z), 0.0)
            a = jnp.exp(jnp.where(earlier, z - sp - c, NEG))
            acc = acc + a * ps_ref[tp:tp + 1, 2816:3072]
            c = c + sp
        c_ref[...] = c
        accd_ref[...] = acc

    qcb = qc.astype(BF16)
    qpos = (past_len + tc_col).astype(F32)
    col = lax.broadcasted_iota(jnp.int32, (1, PAGE * C_KV), 1)
    own_head = (col % C_KV) == (r // hrows)
    ckey = col // C_KV
    ss = []
    for i in range(pp):
        page = npages - 1 - (j * pp + i)
        kpos = (page * PAGE + ckey).astype(F32)
        sc = _nt(qcb, kc_refs[i][...].astype(BF16))
        ss.append(jnp.where(own_head, sc - slope * (qpos - kpos), NEG))
    s = jnp.concatenate(ss, axis=1)
    m_old = m_ref[...]
    m_new = jnp.maximum(m_old, jnp.max(s, axis=-1, keepdims=True))
    alpha = jnp.exp(m_old - m_new)
    p = jnp.exp(s - m_new)
    l_ref[...] = alpha * l_ref[...] + jnp.sum(p, axis=-1, keepdims=True)
    p = p.astype(BF16)
    w = PAGE * C_KV
    pv = _mm(p[:, 0:w], vc_refs[0][...].astype(BF16))
    for i in range(1, pp):
        pv = pv + _mm(p[:, i * w:(i + 1) * w], vc_refs[i][...].astype(BF16))
    accc_ref[...] = alpha * accc_ref[...] + pv
    m_ref[...] = m_new

    qdb = qd.astype(BF16)
    kd_all = jnp.concatenate([kd_refs[i][...].astype(BF16) for i in range(pp)], axis=1)
    z = _mm(qdb, kd_all)
    sp = _softplus(z)
    hi = sp.astype(BF16)
    lo = (sp - hi.astype(F32)).astype(BF16)
    stacked = jnp.concatenate(
        [jnp.concatenate([hi[:, i * PAGE:(i + 1) * PAGE], lo[:, i * PAGE:(i + 1) * PAGE]], axis=1)
         for i in range(pp)], axis=0)
    tl_all = _mm(stacked, uu_ref[...])
    c = c_ref[...]
    acc = accd_ref[...]
    for i in range(pp):
        tl = tl_all[i * QROWS:(i + 1) * QROWS]
        a = jnp.exp(z[:, i * PAGE:(i + 1) * PAGE] - tl - c)
        acc = acc + _nt(a.astype(BF16), vd_refs[i][...].astype(BF16))
        c = c + tl[:, 0:1]
    c_ref[...] = c
    accd_ref[...] = acc

    @pl.when(j == pl.num_programs(1) - 1)
    def _():
        lam = _lambda(lamv_ref[...], lambda_init)
        o = accc_ref[...] / l_ref[...]
        half = t_len * C_GROUP
        for hk in range(C_KV):
            o0 = o[(hk * 2) * half:(hk * 2 + 1) * half, :]
            o1 = o[(hk * 2 + 1) * half:(hk * 2 + 2) * half, :]
            oc_ref[hk * half:(hk + 1) * half, :] = _sub_rms(o0, o1, lam, g_ref[...], lambda_init)
        per = t_len * D_GROUP
        accd = accd_ref[...]
        for hk in range(D_KV):
            od_ref[hk * per:(hk + 1) * per, :] = accd[hk * per:(hk + 1) * per, hk * HD:(hk + 1) * HD]


def _sample_odd(ps, pools, page_base, page_table, lamv, subln_g, lambda_init, t_len=4):
    n, npages = page_table.shape
    pp = PAGES_PER_STEP
    ps3 = ps.reshape(n, t_len, ps.shape[1])
    eye2 = jnp.eye(2, dtype=F32)
    eye4 = jnp.eye(4, dtype=F32)
    qc = ps3[:, :, 0:1024].reshape(n, t_len, C_KV, C_GROUP, 2, HD).transpose(0, 2, 4, 1, 3, 5)
    qc = (qc[:, :, :, :, :, None, :] * eye2[None, None, :, None, None, :, None]).reshape(n, QROWS, 128)
    qd = ps3[:, :, 1536:2560].reshape(n, t_len, D_KV, D_GROUP, HD).transpose(0, 2, 1, 3, 4)
    qd = (qd[:, :, :, :, None, :] * eye4[None, :, None, None, :, None]).reshape(n, QROWS, 256)
    pt = page_table + page_base

    nsteps = npages // pp
    assert nsteps % 2 == 0, "the page double buffer keys its slot on the step parity"
    per_b = lambda w, rws: pl.BlockSpec((None, rws, w), lambda b, j, pt_ref: (b, 0, 0))
    fixed = lambda shape: pl.BlockSpec(shape, lambda b, j, pt_ref: (0, 0))
    in_specs = [per_b(128, QROWS), per_b(256, QROWS), per_b(ps.shape[1], t_len)]
    in_specs += [pl.BlockSpec(memory_space=pl.ANY)] * 4
    in_specs += [fixed((2 * PAGE, PAGE)), fixed((4, HD)), fixed((1, 2 * HD))]
    ins = [qc, qd, ps3, *pools, _tri_matrix(PAGE), lamv, subln_g.reshape(1, 2 * HD)]
    page_buf = pltpu.VMEM((2, pp, 256, PAGE), F32)
    oc, od = pl.pallas_call(
        functools.partial(_sample_odd_kernel, npages=npages, t_len=t_len, past_len=npages * PAGE,
                          lambda_init=lambda_init),
        grid_spec=pltpu.PrefetchScalarGridSpec(
            num_scalar_prefetch=1,
            grid=(n, nsteps),
            in_specs=in_specs,
            out_specs=[per_b(2 * HD, C_KV * t_len * C_GROUP), per_b(HD, D_KV * t_len * D_GROUP)],
            scratch_shapes=[page_buf, page_buf, page_buf, page_buf, pltpu.SemaphoreType.DMA((2,)),
                            pltpu.VMEM((QROWS, 1), F32), pltpu.VMEM((QROWS, 1), F32), pltpu.VMEM((QROWS, 128), F32),
                            pltpu.VMEM((QROWS, 1), F32), pltpu.VMEM((QROWS, 256), F32)]),
        out_shape=[jax.ShapeDtypeStruct((n, C_KV * t_len * C_GROUP, 2 * HD), F32),
                   jax.ShapeDtypeStruct((n, D_KV * t_len * D_GROUP, HD), F32)],
        compiler_params=_params(("arbitrary", "arbitrary")),
        name="sample_odd_attn",
    )(pt, *ins)
    oc = oc.reshape(n, C_KV, t_len, C_GROUP, 2 * HD).transpose(0, 2, 1, 3, 4).reshape(n * t_len, C_HEADS * 2 * HD)
    od = od.reshape(n, D_KV, t_len, D_GROUP, HD).transpose(0, 2, 1, 3, 4).reshape(n * t_len, D_HEADS * HD)
    return oc, od


def kernel(x_prompt, x_sample, cache_a1, cache_a2, cache_a3, cache_b, cache_c_k, cache_c_v, cache_d_k, cache_d_v,
           state_conv, page_table, w_in_even, w_out_even, sinks_b, w_in_odd, w_out_odd,
           lam_q1, lam_k1, lam_q2, lam_k2, subln_g, w_ffn_a, conv_w, conv_b, w_ffn_g, w_ffn_down,
           ln_mix_g, ln_mix_b, ln_ffn_g, ln_ffn_b):
    bsz, seq, d = x_prompt.shape
    nb, t_len, _ = x_sample.shape
    assert bsz == 1, "the prompt group is one sequence"
    xp = x_prompt.reshape(seq, d)
    xs = x_sample.reshape(nb * t_len, d)
    b_off = [(h // (B_HEADS // B_KV)) * HD for h in range(B_HEADS)]
    n_phys = cache_c_k.shape[1]
    pools = [c.reshape(c.shape[0] * n_phys, PAGE * C_KV, 2 * HD) for c in (cache_c_k, cache_c_v)]
    pools += [jnp.transpose(c, (0, 1, 3, 4, 2)).reshape(c.shape[0] * n_phys, D_KV * HD, PAGE)
              for c in (cache_d_k, cache_d_v)]
    even_p, even_s, odd_p, odd_s, conv_p, conv_s = [], [], [], [], [], []
    for layer in range(DEPTH):
        i = layer // 2
        if layer % 2 == 0:
            w_in = jnp.pad(w_in_even[i].astype(BF16), ((0, 0), (0, EVEN_PAD - EVEN_IN)))
            w_out = w_out_even[i].astype(BF16)
            pp = _matmul(xp, w_in, 512, 1408)
            ps = _matmul(xs, w_in, 512, 1408)
            prev = None
            for g, (_, dil) in enumerate(A_PATTERNS):
                last = g == len(A_PATTERNS) - 1
                res = _band_prompt(pp, dil, 4 if dil == 1 else 1, g * 512, 1536 + g * 512, 3072 + g * 512,
                                   2, 128, (0, HD), head_step=0.25, prev=prev, want_lse=not last)
                prev = None if last else res
            oa_p = res
            ob_p = _band_prompt(pp, 1, 4, 4608, 5120, 5248, B_HEADS, 128, b_off, sinks=sinks_b[i].reshape(B_HEADS))
            oa_s, ob_s = _sample_even(ps, cache_a1, cache_a2, cache_a3, cache_b, i, sinks_b[i].reshape(B_HEADS), t_len)
            w1, w2 = w_out[:512], w_out[512:]
            rows_p, rows_s = [], []
            for g, (win, _) in enumerate(A_PATTERNS):
                w = min(win, seq)
                kv = jnp.stack([pp[seq - w:, 1536 + g * 512:2048 + g * 512].reshape(w, A_HEADS, HD),
                                pp[seq - w:, 3072 + g * 512:3584 + g * 512].reshape(w, A_HEADS, HD)], axis=1)
                rows_p.append(kv[None])
                rows_s.append(jnp.stack([ps[:, 1536 + g * 512:2048 + g * 512].reshape(nb, t_len, A_HEADS, HD),
                                         ps[:, 3072 + g * 512:3584 + g * 512].reshape(nb, t_len, A_HEADS, HD)], axis=2))
            w = min(128, seq)
            rows_p.append(jnp.stack([pp[seq - w:, 5120:5248].reshape(w, B_KV, HD),
                                     pp[seq - w:, 5248:5376].reshape(w, B_KV, HD)], axis=1)[None])
            rows_s.append(jnp.stack([ps[:, 5120:5248].reshape(nb, t_len, B_KV, HD),
                                     ps[:, 5248:5376].reshape(nb, t_len, B_KV, HD)], axis=2))
            even_p.append(rows_p)
            even_s.append(rows_s)
            mix_p, mix_s = (oa_p, ob_p), (oa_s, ob_s)
        else:
            lambda_init = 0.8 - 0.6 * math.exp(-0.3 * layer)
            lamv = jnp.stack([lam_q1[i], lam_k1[i], lam_q2[i], lam_k2[i]])
            w_in = w_in_odd[i].astype(BF16)
            w_out = w_out_odd[i].astype(BF16)
            pp = _matmul(xp, w_in, 512, 1024)
            ps = _matmul(xs, w_in, 512, 1024)
            wvt = jnp.concatenate([w_in[:, 1280:1536], w_in[:, 2816:3072]], axis=1).T
            vt = _matmul_t(wvt, xp)
            oc_p = _diff_prompt(pp, vt, lamv, subln_g[i], lambda_init)
            od_p = _sb_prompt(pp, vt)
            oc_s, od_s = _sample_odd(ps, pools, i * n_phys, page_table, lamv, subln_g[i], lambda_init, t_len)
            w1, w2 = w_out[:1024], w_out[1024:]
            odd_p.append((pp[:, 1024:1280].reshape(1, seq, C_KV, 2 * HD), pp[:, 1280:1536].reshape(1, seq, C_KV, 2 * HD),
                          pp[:, 2560:2816].reshape(1, seq, D_KV, HD), pp[:, 2816:3072].reshape(1, seq, D_KV, HD)))
            odd_s.append((ps[:, 1024:1280].reshape(nb, t_len, C_KV, 2 * HD), ps[:, 1280:1536].reshape(nb, t_len, C_KV, 2 * HD),
                          ps[:, 2560:2816].reshape(nb, t_len, D_KV, HD), ps[:, 2816:3072].reshape(nb, t_len, D_KV, HD)))
            mix_p, mix_s = (oc_p, od_p), (oc_s, od_s)
        xp = _outproj_ln(mix_p[0], mix_p[1], w1, w2, xp, ln_mix_g[layer], ln_mix_b[layer])
        xs = _outproj_ln(mix_s[0], mix_s[1], w1, w2, xs, ln_mix_g[layer], ln_mix_b[layer])
        wa, wg, wd = w_ffn_a[layer].astype(BF16), w_ffn_g[layer].astype(BF16), w_ffn_down[layer].astype(BF16)
        ffn_args = (wa, wg, wd, conv_w[layer], conv_b[layer], ln_ffn_g[layer], ln_ffn_b[layer])
        xp, tail_p = _ffn(xp, *ffn_args)
        st = state_conv[layer]
        p1 = jnp.concatenate([st[:, 1:2], jnp.zeros((nb, t_len - 1, D_FF), F32)], axis=1).reshape(nb * t_len, D_FF)
        p2 = jnp.concatenate([st, jnp.zeros((nb, t_len - 2, D_FF), F32)], axis=1).reshape(nb * t_len, D_FF)
        xs, tail_s = _ffn(xs, *ffn_args, prev=(p1, p2), t_len=t_len)
        conv_p.append(tail_p[-2:][None])
        conv_s.append(tail_s.reshape(nb, t_len, D_FF)[:, t_len - 2:])
    stack = lambda per_layer, j: jnp.stack([entry[j] for entry in per_layer])
    outs = [xp.reshape(1, seq, d), xs.reshape(nb, t_len, d)]
    for j in range(4):
        outs += [stack(even_p, j), stack(even_s, j)]
    for j in range(4):
        outs += [stack(odd_p, j), stack(odd_s, j)]
    outs += [jnp.stack(conv_p), jnp.stack(conv_s)]
    return tuple(outs)
```

```python
import functools
import math

import numpy as np
import jax
import jax.numpy as jnp
from jax import lax
from jax.experimental import pallas as pl
from jax.experimental.pallas import tpu as pltpu

F32 = jnp.float32
BF16 = jnp.bfloat16

D_MODEL = 2048
DEPTH = 2
PAGE = 128
HD = 64
BLK = 128
A_PATTERNS = ((128, 1), (512, 4), (2048, 16))
A_HEADS = 8
B_HEADS = 8
B_KV = 2
C_HEADS = 8
C_KV = 2
D_HEADS = 16
D_KV = 4
D_FF = 5632
LN_EPS = 1e-5
NEG = -1e30
ALPHA = (2 * DEPTH) ** 0.25
EVEN_IN = 5376
EVEN_PAD = 5632
ODD_IN = 3072
SCALE = HD ** -0.5

VMEM_LIMIT = 56 * 1024 * 1024


def _params(sem):
    return pltpu.CompilerParams(dimension_semantics=sem, vmem_limit_bytes=VMEM_LIMIT)


def _nt(a, b):
    return lax.dot_general(a, b, (((1,), (1,)), ((), ())), preferred_element_type=F32)


def _mm(a, b):
    return jnp.dot(a, b, preferred_element_type=F32)


def _slope(h, n=8):
    return 2.0 ** (-8.0 * (h + 1) / n)


def _matmul_kernel(x_ref, w_ref, o_ref, xb_ref):
    @pl.when(pl.program_id(1) == 0)
    def _():
        xb_ref[...] = x_ref[...].astype(BF16)

    o_ref[...] = _mm(xb_ref[...], w_ref[...])


def _matmul(x, w, tm, tn):
    m, k = x.shape
    n = w.shape[1]
    tm = min(tm, m)
    return pl.pallas_call(
        _matmul_kernel,
        grid=(m // tm, n // tn),
        in_specs=[pl.BlockSpec((tm, k), lambda i, j: (i, 0)),
                  pl.BlockSpec((k, tn), lambda i, j: (0, j))],
        out_specs=pl.BlockSpec((tm, tn), lambda i, j: (i, j)),
        out_shape=jax.ShapeDtypeStruct((m, n), F32),
        scratch_shapes=[pltpu.VMEM((tm, k), BF16)],
        compiler_params=_params(("parallel", "arbitrary")),
        name="proj_matmul",
    )(x, w)


def _layer_norm_rows(r, g, b):
    mu = jnp.mean(r, axis=-1, keepdims=True)
    d = r - mu
    var = jnp.mean(d * d, axis=-1, keepdims=True)
    return d * lax.rsqrt(var + LN_EPS) * g + b


def _outproj_ln_kernel(m1_ref, m2_ref, w1_ref, w2_ref, x_ref, g_ref, b_ref, o_ref):
    y = _mm(m1_ref[...].astype(BF16), w1_ref[...]) + _mm(m2_ref[...].astype(BF16), w2_ref[...])
    o_ref[...] = _layer_norm_rows(ALPHA * x_ref[...] + y, g_ref[...], b_ref[...])


def _outproj_ln(m1, m2, w1, w2, x, g, b, tm=256):
    m, d = x.shape
    k1, k2 = m1.shape[1], m2.shape[1]
    tm = min(tm, m)
    row = lambda i: (i, 0)
    fixed = lambda i: (0, 0)
    return pl.pallas_call(
        _outproj_ln_kernel,
        grid=(m // tm,),
        in_specs=[pl.BlockSpec((tm, k1), row), pl.BlockSpec((tm, k2), row),
                  pl.BlockSpec((k1, d), fixed), pl.BlockSpec((k2, d), fixed),
                  pl.BlockSpec((tm, d), row), pl.BlockSpec((1, d), fixed), pl.BlockSpec((1, d), fixed)],
        out_specs=pl.BlockSpec((tm, d), row),
        out_shape=jax.ShapeDtypeStruct((m, d), F32),
        compiler_params=_params(("parallel",)),
        name="outproj_ln",
    )(m1, m2, w1, w2, x, g.reshape(1, d), b.reshape(1, d))


FFN_HALO = 16


def _ffn_kernel(*refs, tm, sample, t_len):
    if sample:
        x_ref, p1_ref, p2_ref, wa_ref, wg_ref, wd_ref, cw_ref, cb_ref, g_ref, b_ref, o_ref, tail_ref, xb_ref, acc_ref = refs
    else:
        x_ref, xh_ref, wa_ref, wg_ref, wd_ref, cw_ref, cb_ref, g_ref, b_ref, o_ref, tail_ref, xb_ref, acc_ref = refs
    i = pl.program_id(0)
    f = pl.program_id(1)

    @pl.when(f == 0)
    def _():
        xb_ref[FFN_HALO:, :] = x_ref[...].astype(BF16)
        if sample:
            xb_ref[:FFN_HALO, :] = jnp.zeros((FFN_HALO, x_ref.shape[1]), BF16)
        else:
            halo = jnp.where(i > 0, xh_ref[...], 0.0)
            xb_ref[:FFN_HALO, :] = halo.astype(BF16)
        acc_ref[...] = jnp.zeros_like(acc_ref)

    a_ext = _mm(xb_ref[...], wa_ref[...])
    gate = _mm(xb_ref[FFN_HALO:, :], wg_ref[...])
    a = a_ext[FFN_HALO:, :]
    a1 = pltpu.roll(a_ext, 1, 0)[FFN_HALO:, :]
    a2 = pltpu.roll(a_ext, 2, 0)[FFN_HALO:, :]
    if sample:
        t = lax.broadcasted_iota(jnp.int32, a.shape, 0) % t_len
        a1 = jnp.where(t == 0, p1_ref[...], a1)
        a2 = jnp.where(t < 2, p2_ref[...], a2)
    cw = cw_ref[...]
    c = a2 * cw[0:1, :] + a1 * cw[1:2, :] + a * cw[2:3, :] + cb_ref[...]
    h = (c * jax.nn.sigmoid(c)) * gate
    acc_ref[...] += _mm(h.astype(BF16), wd_ref[...])
    tail_ref[...] = a[tm - tail_ref.shape[0]:, :]

    @pl.when(f == pl.num_programs(1) - 1)
    def _():
        o_ref[...] = _layer_norm_rows(ALPHA * x_ref[...] + acc_ref[...], g_ref[...], b_ref[...])


def _ffn(x, wa, wg, wd, cw, cb, g, b, prev=None, t_len=4, tm=512, tf=512):
    m, d = x.shape
    dff = wa.shape[1]
    tm = min(tm, m)
    sample = prev is not None
    row = lambda i, f: (i, 0)
    col = lambda i, f: (0, f)
    wcol = pl.BlockSpec((d, tf), col)
    ins = [x]
    specs = [pl.BlockSpec((tm, d), row)]
    if sample:
        ins += list(prev)
        specs += [pl.BlockSpec((tm, tf), lambda i, f: (i, f))] * 2
        tail_rows = tm
        tail_spec = pl.BlockSpec((tm, tf), lambda i, f: (i, f))
        tail_shape = (m, dff)
    else:
        ins += [x]
        hb = tm // FFN_HALO
        specs += [pl.BlockSpec((FFN_HALO, d), lambda i, f: (jnp.maximum(i * hb - 1, 0), 0))]
        tail_rows = 8
        tail_spec = pl.BlockSpec((8, tf), lambda i, f: (i, f))
        tail_shape = (8 * (m // tm), dff)
    ins += [wa, wg, wd, cw, cb.reshape(1, dff), g.reshape(1, d), b.reshape(1, d)]
    specs += [wcol, wcol, pl.BlockSpec((tf, d), lambda i, f: (f, 0)),
              pl.BlockSpec((3, tf), col), pl.BlockSpec((1, tf), col),
              pl.BlockSpec((1, d), lambda i, f: (0, 0)), pl.BlockSpec((1, d), lambda i, f: (0, 0))]
    return pl.pallas_call(
        functools.partial(_ffn_kernel, tm=tm, sample=sample, t_len=t_len),
        grid=(m // tm, dff // tf),
        in_specs=specs,
        out_specs=[pl.BlockSpec((tm, d), row), tail_spec],
        out_shape=[jax.ShapeDtypeStruct((m, d), F32), jax.ShapeDtypeStruct(tail_shape, F32)],
        scratch_shapes=[pltpu.VMEM((tm + FFN_HALO, d), BF16), pltpu.VMEM((tm, d), F32)],
        compiler_params=_params(("arbitrary", "arbitrary")),
        name="conv_ffn_sample" if sample else "conv_ffn_prompt",
    )(*ins)


def _band_prompt_kernel(*refs, dil, sub, kv_off, head_step, has_sink, has_prev, want_lse):
    it = iter(refs)
    q_ref, kc_ref, kp_ref, vc_ref, vp_ref = (next(it) for _ in range(5))
    sink_ref = next(it) if has_sink else None
    op_ref, lp_ref = (next(it), next(it)) if has_prev else (None, None)
    o_ref = next(it)
    l_ref = next(it) if want_lse else None
    c = pl.program_id(0)
    hg = pl.program_id(1)
    nh = len(kv_off)
    group_scale = jnp.float32(1.0)
    for n in range(1, 4):
        group_scale = jnp.where(hg == n, jnp.float32(head_step ** n), group_scale)
    qi = lax.broadcasted_iota(jnp.int32, (BLK, BLK), 0)
    kj = lax.broadcasted_iota(jnp.int32, (BLK, BLK), 1)
    mask_c = kj <= qi
    mask_p = kj >= qi
    dist_c = (qi - kj).astype(F32) * float(dil)
    dist_p = (BLK + qi - kj).astype(F32) * float(dil)

    def rows(start):
        return pl.ds(start, BLK, stride=dil) if dil > 1 else pl.ds(start, BLK)

    for res in range(dil):
        for w in range(sub):
            cur = rows(res + BLK * w * dil)
            q_all = q_ref[cur, :]
            kc_all, vc_all = kc_ref[cur, :], vc_ref[cur, :]
            if w == 0:
                kp_all, vp_all = kp_ref[rows(res), :], vp_ref[rows(res), :]
                mp = jnp.logical_and(mask_p, c > 0)
            else:
                prv = rows(res + BLK * (w - 1) * dil)
                kp_all, vp_all = kc_ref[prv, :], vc_ref[prv, :]
                mp = mask_p
            if has_prev:
                op_all, lp_all = op_ref[cur, :], lp_ref[cur, :]
            o_parts, l_parts = [], []
            for h in range(nh):
                hs = slice(h * HD, (h + 1) * HD)
                ks = slice(kv_off[h], kv_off[h] + HD)
                slope = _slope(h) * group_scale
                q = q_all[:, hs].astype(BF16)
                s_c = _nt(q, kc_all[:, ks].astype(BF16)) * SCALE - slope * dist_c
                s_p = _nt(q, kp_all[:, ks].astype(BF16)) * SCALE - slope * dist_p
                s_c = jnp.where(mask_c, s_c, NEG)
                s_p = jnp.where(mp, s_p, NEG)
                m = jnp.maximum(jnp.max(s_c, axis=-1, keepdims=True), jnp.max(s_p, axis=-1, keepdims=True))
                p_c = jnp.exp(s_c - m)
                p_p = jnp.exp(s_p - m)
                l = jnp.sum(p_c, axis=-1, keepdims=True) + jnp.sum(p_p, axis=-1, keepdims=True)
                if has_sink:
                    l = l + jnp.exp(sink_ref[h] - m)
                o = (_mm(p_c.astype(BF16), vc_all[:, ks].astype(BF16))
                     + _mm(p_p.astype(BF16), vp_all[:, ks].astype(BF16)))
                o = o / l
                lse = m + jnp.log(l)
                if has_prev:
                    lse_prev = lp_all[:, hs]
                    lse_new = jnp.logaddexp(lse_prev, lse)
                    o = op_all[:, hs] * jnp.exp(lse_prev - lse_new) + o * jnp.exp(lse - lse_new)
                    lse = lse_new
                o_parts.append(o)
                l_parts.append(jnp.broadcast_to(lse, (BLK, HD)))
            o_ref[cur, :] = jnp.concatenate(o_parts, axis=1)
            if want_lse:
                l_ref[cur, :] = jnp.concatenate(l_parts, axis=1)


def _band_prompt(proj, dil, sub, q_col, k_col, v_col, nh, kv_w, kv_off, head_step=1.0, sinks=None, prev=None,
                 want_lse=False):
    s, w = proj.shape
    chunk = BLK * dil * sub
    tail = BLK * dil
    qw = nh * HD
    ngroups = 512 // qw
    kv_groups = 1 if kv_w == 128 and nh == 8 else ngroups
    kcol = (lambda g: k_col // kv_w + g) if kv_groups > 1 else (lambda g: k_col // kv_w)
    vcol = (lambda g: v_col // kv_w + g) if kv_groups > 1 else (lambda g: v_col // kv_w)
    qspec = pl.BlockSpec((chunk, qw), lambda c, g: (c, q_col // qw + g))
    kc = pl.BlockSpec((chunk, kv_w), lambda c, g: (c, kcol(g)))
    kp = pl.BlockSpec((tail, kv_w), lambda c, g: (jnp.maximum(c * sub - 1, 0), kcol(g)))
    vc = pl.BlockSpec((chunk, kv_w), lambda c, g: (c, vcol(g)))
    vp = pl.BlockSpec((tail, kv_w), lambda c, g: (jnp.maximum(c * sub - 1, 0), vcol(g)))
    ospec = pl.BlockSpec((chunk, qw), lambda c, g: (c, g))
    ins = [proj, proj, proj, proj, proj]
    specs = [qspec, kc, kp, vc, vp]
    if sinks is not None:
        ins.append(sinks)
        specs.append(pl.BlockSpec(memory_space=pltpu.SMEM))
    if prev is not None:
        ins += [prev[0], prev[1]]
        specs += [ospec, ospec]
    oshape = jax.ShapeDtypeStruct((s, 512), F32)
    return pl.pallas_call(
        functools.partial(_band_prompt_kernel, dil=dil, sub=sub, kv_off=tuple(kv_off), head_step=head_step,
                          has_sink=sinks is not None, has_prev=prev is not None, want_lse=want_lse),
        grid=(s // chunk, ngroups),
        in_specs=specs,
        out_specs=[ospec, ospec] if want_lse else ospec,
        out_shape=[oshape, oshape] if want_lse else oshape,
        compiler_params=_params(("parallel", "parallel")),
        name=f"band_prompt_d{dil}",
    )(*ins)


def _col_consts(values, rows):
    r = lax.broadcasted_iota(jnp.int32, (rows, 1), 0)
    col = jnp.zeros((rows, 1), F32)
    for h, v in enumerate(values):
        col = jnp.where(r == h, v, col)
    return col


def _sample_even_kernel(ps_ref, a1_ref, a2_ref, a3_ref, cb_ref, fold_ref, sink_ref, oa_ref, ob_ref, *, t_len):
    nh = 8
    nr = t_len * nh
    row = lax.broadcasted_iota(jnp.int32, (nr, 1), 0)
    t_col = row // nh
    h_col = row % nh
    hm = (lax.broadcasted_iota(jnp.int32, (nr, nh * HD), 1) // HD == h_col).astype(F32)
    slope = jnp.zeros((nr, 1), F32)
    sink_col = jnp.zeros((nr, 1), F32)
    for h in range(nh):
        slope = jnp.where(h_col == h, _slope(h), slope)
        sink_col = jnp.where(h_col == h, sink_ref[h], sink_col)

    def q_rows(col):
        return jnp.concatenate([jnp.broadcast_to(ps_ref[t:t + 1, col:col + 512], (nh, 512))
                                for t in range(t_len)], axis=0) * hm

    def attend(qk, q_new, kt_ref_slice, vt_ref_slice, k_col, v_col, width, dil, sinks):
        kt = kt_ref_slice.astype(BF16)
        length = kt.shape[1]
        dd = length + t_col - lax.broadcasted_iota(jnp.int32, (nr, length), 1)
        valid = jnp.logical_and((dd & (dil - 1)) == 0, dd <= dil * BLK)
        s = jnp.where(valid, _mm(qk, kt) * SCALE - slope * dd.astype(F32), NEG)
        m = jnp.max(s, axis=-1, keepdims=True)
        new = []
        for tp in range(t_len):
            dn = t_col - tp
            ok = jnp.logical_and(dn >= 0, (dn & (dil - 1)) == 0)
            sc = jnp.sum(q_new * ps_ref[tp:tp + 1, k_col:k_col + width], axis=-1, keepdims=True) * SCALE
            sc = jnp.where(ok, sc - slope * dn.astype(F32), NEG)
            new.append(sc)
            m = jnp.maximum(m, sc)
        p = jnp.exp(s - m)
        l = jnp.sum(p, axis=-1, keepdims=True)
        o = _nt(p.astype(BF16), vt_ref_slice.astype(BF16))
        for tp in range(t_len):
            pn = jnp.exp(new[tp] - m)
            l = l + pn
            o = o + pn * ps_ref[tp:tp + 1, v_col:v_col + width]
        if sinks:
            l = l + jnp.exp(sink_col - m)
        return o / l, m + jnp.log(l)

    def head_rows(o_wide):
        return jnp.sum((o_wide * hm).reshape(t_len, nh, nh * HD), axis=1)

    outs, lses = [], []
    for g, (buf, (_, dil)) in enumerate(zip((a1_ref, a2_ref, a3_ref), A_PATTERNS)):
        q = q_rows(g * 512)
        o, lse = attend(q.astype(BF16), q, buf[0:512, :], buf[512:1024, :],
                        1536 + g * 512, 3072 + g * 512, 512, dil, False)
        outs.append(o)
        lses.append(lse)
    mx = jnp.maximum(jnp.maximum(lses[0], lses[1]), lses[2])
    es = [jnp.exp(l_ - mx) for l_ in lses]
    merged = (es[0] * outs[0] + es[1] * outs[1] + es[2] * outs[2]) / (es[0] + es[1] + es[2])
    oa_ref[...] = head_rows(merged)

    qf = _mm(q_rows(4608).astype(BF16), fold_ref[...])
    o, _ = attend(qf.astype(BF16), qf, cb_ref[0:128, :], cb_ref[128:256, :], 5120, 5248, 128, 1, True)
    o_sw = pltpu.roll(o, HD, 1)
    o_sel = jnp.where((h_col // 4) == (h_col % 2), o, o_sw)
    ob_ref[...] = head_rows(jnp.concatenate([o_sel] * 4, axis=1))


def _fold_matrix():
    c = np.arange(512)[:, None]
    l = np.arange(128)[None, :]
    return jnp.asarray(((c % HD == l % HD) & (l // HD == (c // HD) // 4)).astype(np.float32), BF16)


def _sample_even(ps, a1, a2, a3, cb, layer, sinks, t_len=4):
    nl, n = a1.shape[0], a1.shape[1]
    ps3 = ps.reshape(n, t_len, ps.shape[1])

    def rows_last(c):
        return jnp.transpose(c, (0, 1, 3, 4, 5, 2)).reshape(nl * n, -1, c.shape[2])

    a1v, a2v, a3v, cbv = rows_last(a1), rows_last(a2), rows_last(a3), rows_last(cb)
    b3 = lambda arr: pl.BlockSpec((None,) + arr.shape[1:], lambda i: (layer * n + i, 0, 0))
    ospec = pl.BlockSpec((None, t_len, 512), lambda i: (i, 0, 0))
    oa, ob = pl.pallas_call(
        functools.partial(_sample_even_kernel, t_len=t_len),
        grid=(n,),
        in_specs=[pl.BlockSpec((None, t_len, ps.shape[1]), lambda i: (i, 0, 0)),
                  b3(a1v), b3(a2v), b3(a3v), b3(cbv),
                  pl.BlockSpec((512, 128), lambda i: (0, 0)),
                  pl.BlockSpec(memory_space=pltpu.SMEM)],
        out_specs=[ospec, ospec],
        out_shape=[jax.ShapeDtypeStruct((n, t_len, 512), F32)] * 2,
        compiler_params=_params(("parallel",)),
        name="sample_even_attn",
    )(ps3, a1v, a2v, a3v, cbv, _fold_matrix(), sinks)
    return oa.reshape(n * t_len, 512), ob.reshape(n * t_len, 512)


C_GROUP = C_HEADS // C_KV
D_GROUP = D_HEADS // D_KV


def _lambda(lamv, lambda_init):
    s1 = jnp.sum(lamv[0:1, :] * lamv[1:2, :], axis=-1, keepdims=True)
    s2 = jnp.sum(lamv[2:3, :] * lamv[3:4, :], axis=-1, keepdims=True)
    return jnp.exp(s1) - jnp.exp(s2) + lambda_init


def _sub_rms(o0, o1, lam, g, lambda_init):
    d = o0 - lam * o1
    return d * lax.rsqrt(jnp.mean(d * d, axis=-1, keepdims=True) + LN_EPS) * g * (1.0 - lambda_init)


def _softplus(z):
    return jnp.maximum(z, 0.0) + jnp.log(1.0 + jnp.exp2(jnp.abs(z) * (-math.log2(math.e))))


def _tri_matrix(n):
    u = (np.arange(n)[:, None] >= np.arange(n)[None, :]).astype(np.float32)
    return jnp.asarray(np.concatenate([u, u], axis=0), BF16)


def _rev_cumsum(sp, uu):
    hi = sp.astype(BF16)
    lo = (sp - hi.astype(F32)).astype(BF16)
    return _mm(jnp.concatenate([hi, lo], axis=1), uu)


def _matmul_t_kernel(wt_ref, x_ref, o_ref):
    o_ref[...] = _nt(wt_ref[...], x_ref[...].astype(BF16))


def _matmul_t(wt, x, tm=512):
    n, k = wt.shape
    m = x.shape[0]
    tm = min(tm, m)
    return pl.pallas_call(
        _matmul_t_kernel,
        grid=(m // tm,),
        in_specs=[pl.BlockSpec((n, k), lambda i: (0, 0)), pl.BlockSpec((tm, k), lambda i: (i, 0))],
        out_specs=pl.BlockSpec((n, tm), lambda i: (0, i)),
        out_shape=jax.ShapeDtypeStruct((n, m), F32),
        compiler_params=_params(("parallel",)),
        name="proj_matmul_t",
    )(wt, x)


def _diff_prompt_kernel(q_ref, k_ref, vt_ref, lamv_ref, g_ref, o_ref, b0_ref, m_ref, l_ref, acc_ref, *, tq, lambda_init):
    hk = pl.program_id(0)
    i = pl.program_id(1)
    rows = C_GROUP * tq
    lane = lax.broadcasted_iota(jnp.int32, (1, rows), 1)
    slope = jnp.zeros((1, rows), F32)
    for g in range(C_GROUP):
        slope = jnp.where(lane // tq == g, _slope(g), slope)
    slope = slope * jnp.where(hk == 0, 1.0, 2.0 ** (-C_GROUP)).astype(F32)
    kl = lax.broadcasted_iota(jnp.int32, (tq, rows), 0)
    ql = lax.broadcasted_iota(jnp.int32, (tq, rows), 1) % tq
    b0_ref[...] = slope * (kl - ql).astype(F32)
    qs = []
    for m in range(2):
        parts = [q_ref[:, (g * 2 + m) * HD:(g * 2 + m + 1) * HD] for g in range(C_GROUP)]
        qs.append((jnp.concatenate(parts, axis=0) * SCALE).astype(BF16))
    m_ref[...] = jnp.full(m_ref.shape, NEG, F32)
    l_ref[...] = jnp.zeros_like(l_ref)
    acc_ref[...] = jnp.zeros_like(acc_ref)

    def blocks(js, diagonal):
        ks, vts, cjs = [], [], []
        for j in js:
            start = pl.multiple_of(j * tq, tq)
            ks.append(k_ref[pl.ds(start, tq), :].astype(BF16))
            vts.append(vt_ref[:, pl.ds(start, tq)].astype(BF16))
            cjs.append(-slope * ((i - j) * tq).astype(F32))
        for m in range(2):
            ss = []
            m_old = m_ref[m]
            m_new = m_old
            for k, cj in zip(ks, cjs):
                s = _nt(k[:, m * HD:(m + 1) * HD], qs[m]) + b0_ref[...]
                if diagonal:
                    s = jnp.where(kl <= ql, s, NEG)
                ss.append(s)
                m_new = jnp.maximum(m_new, jnp.max(s, axis=0, keepdims=True) + cj)
            alpha = jnp.exp(m_old - m_new)
            l = alpha * l_ref[m]
            acc = alpha * acc_ref[m]
            for s, vt, cj in zip(ss, vts, cjs):
                p = jnp.exp(s + (cj - m_new))
                l = l + jnp.sum(p, axis=0, keepdims=True)
                acc = acc + _mm(vt, p.astype(BF16))
            l_ref[m] = l
            acc_ref[m] = acc
            m_ref[m] = m_new

    @pl.when(i % 2 == 1)
    def _():
        blocks((0,), False)

    def off_diagonal(it, carry):
        j = i % 2 + 2 * it
        blocks((j, j + 1), False)
        return carry

    lax.fori_loop(0, i // 2, off_diagonal, 0)
    blocks((i,), True)
    lam = _lambda(lamv_ref[...], lambda_init)
    d = acc_ref[0] / l_ref[0] - lam * (acc_ref[1] / l_ref[1])
    d = d * lax.rsqrt(jnp.mean(d * d, axis=0, keepdims=True) + LN_EPS) * g_ref[...] * (1.0 - lambda_init)
    for g in range(C_GROUP):
        o_ref[:, g * 2 * HD:(g + 1) * 2 * HD] = d[:, g * tq:(g + 1) * tq].T


def _diff_prompt(proj, vt, lamv, subln_g, lambda_init, tq=256):
    s = proj.shape[0]
    rows = C_GROUP * tq
    return pl.pallas_call(
        functools.partial(_diff_prompt_kernel, tq=tq, lambda_init=lambda_init),
        grid=(C_KV, s // tq),
        in_specs=[pl.BlockSpec((tq, 512), lambda h, i: (i, h)),
                  pl.BlockSpec((s, 128), lambda h, i: (0, 8 + h)),
                  pl.BlockSpec((2 * HD, s), lambda h, i: (h, 0)),
                  pl.BlockSpec((4, HD), lambda h, i: (0, 0)),
                  pl.BlockSpec((2 * HD, 1), lambda h, i: (0, 0))],
        out_specs=pl.BlockSpec((tq, 512), lambda h, i: (i, h)),
        out_shape=jax.ShapeDtypeStruct((s, C_HEADS * 2 * HD), F32),
        scratch_shapes=[pltpu.VMEM((tq, rows), F32), pltpu.VMEM((2, 1, rows), F32), pltpu.VMEM((2, 1, rows), F32),
                        pltpu.VMEM((2, 2 * HD, rows), F32)],
        compiler_params=_params(("parallel", "arbitrary")),
        name="diff_attn_prompt",
    )(proj, proj, vt, lamv, subln_g.reshape(2 * HD, 1))


def _tri_matrix_t(n):
    u = (np.arange(n)[None, :] >= np.arange(n)[:, None]).astype(np.float32)
    return jnp.asarray(np.concatenate([u, u], axis=1), BF16)


def _sb_prompt_kernel(q_ref, k_ref, vt_ref, ut_ref, o_ref, c_ref, acc_ref, *, tq):
    i = pl.program_id(1)
    rows = D_GROUP * tq
    kl = lax.broadcasted_iota(jnp.int32, (tq, rows), 0)
    ql = lax.broadcasted_iota(jnp.int32, (tq, rows), 1) % tq
    qs = []
    for hl in range(2):
        parts = [q_ref[:, (hl * D_GROUP + g) * HD:(hl * D_GROUP + g + 1) * HD] for g in range(D_GROUP)]
        qs.append((jnp.concatenate(parts, axis=0) * SCALE).astype(BF16))
    c_ref[...] = jnp.zeros_like(c_ref)
    acc_ref[...] = jnp.zeros_like(acc_ref)

    def blocks(js, diagonal):
        for hl in range(2):
            c = c_ref[hl]
            pv = None
            for j in js:
                start = pl.multiple_of(j * tq, tq)
                k = k_ref[pl.ds(start, tq), hl * HD:(hl + 1) * HD].astype(BF16)
                vt = vt_ref[hl * HD:(hl + 1) * HD, pl.ds(start, tq)].astype(BF16)
                z = _nt(k, qs[hl])
                sp = _softplus(z)
                if diagonal:
                    sp = jnp.where(kl < ql, sp, 0.0)
                hi = sp.astype(BF16)
                lo = (sp - hi.astype(F32)).astype(BF16)
                tl = _mm(ut_ref[...], jnp.concatenate([hi, lo], axis=0))
                e = z - tl - c
                if diagonal:
                    e = jnp.where(kl < ql, e, NEG)
                term = _mm(vt, jnp.exp(e).astype(BF16))
                pv = term if pv is None else pv + term
                c = c + tl[0:1, :]
            acc_ref[hl * HD:(hl + 1) * HD, :] += pv
            c_ref[hl] = c

    blocks((i,), True)

    @pl.when(i % 2 == 1)
    def _():
        blocks((i - 1,), False)

    top = i - 1 - i % 2

    def older(it, carry):
        blocks((top - 2 * it, top - 2 * it - 1), False)
        return carry

    lax.fori_loop(0, i // 2, older, 0)
    for g in range(D_GROUP):
        t = acc_ref[:, g * tq:(g + 1) * tq].T
        for hl in range(2):
            o_ref[:, (hl * D_GROUP + g) * HD:(hl * D_GROUP + g + 1) * HD] = t[:, hl * HD:(hl + 1) * HD]


def _sb_prompt(proj, vt, tq=256):
    s = proj.shape[0]
    rows = D_GROUP * tq
    return pl.pallas_call(
        functools.partial(_sb_prompt_kernel, tq=tq),
        grid=(D_KV // 2, s // tq),
        in_specs=[pl.BlockSpec((tq, 512), lambda h, i: (i, 3 + h)),
                  pl.BlockSpec((s, 128), lambda h, i: (0, 20 + h)),
                  pl.BlockSpec((2 * HD, s), lambda h, i: (2 + h, 0)),
                  pl.BlockSpec((tq, 2 * tq), lambda h, i: (0, 0))],
        out_specs=pl.BlockSpec((tq, 512), lambda h, i: (i, h)),
        out_shape=jax.ShapeDtypeStruct((s, D_HEADS * HD), F32),
        scratch_shapes=[pltpu.VMEM((2, 1, rows), F32), pltpu.VMEM((2 * HD, rows), F32)],
        compiler_params=_params(("parallel", "arbitrary")),
        name="sb_attn_prompt",
    )(proj, proj, vt, _tri_matrix_t(tq))


PAGES_PER_STEP = 8
QROWS = 64


def _sample_odd_kernel(pt_ref, qc_ref, qd_ref, ps_ref, *refs, npages, t_len, past_len, lambda_init):
    pp = PAGES_PER_STEP
    hrows = QROWS // C_KV
    pools = refs[0:4]
    uu_ref, lamv_ref, g_ref, oc_ref, od_ref = refs[4:9]
    bufs = refs[9:13]
    sem, m_ref, l_ref, accc_ref, c_ref, accd_ref = refs[13:]
    b = pl.program_id(0)
    j = pl.program_id(1)
    nsteps = pl.num_programs(1)
    slot = j % 2

    def page_copies(bb, jj, sl):
        out = []
        for i in range(pp):
            page = pt_ref[bb, npages - 1 - (jj * pp + i)]
            for pool, buf in zip(pools, bufs):
                out.append(pltpu.make_async_copy(pool.at[page], buf.at[sl, i], sem.at[sl]))
        return out

    def start_all(copies):
        for n, cp in enumerate(copies):
            cp.start(priority=n % 2)

    @pl.when(jnp.logical_and(b == 0, j == 0))
    def _():
        start_all(page_copies(0, 0, 0))

    last = jnp.logical_and(b == pl.num_programs(0) - 1, j == nsteps - 1)

    @pl.when(jnp.logical_not(last))
    def _():
        nj = jnp.where(j == nsteps - 1, 0, j + 1)
        nb = jnp.where(j == nsteps - 1, b + 1, b)
        start_all(page_copies(nb, nj, 1 - slot))

    for cp in page_copies(b, j, slot):
        cp.wait()
    kc_refs = [bufs[0].at[slot, i] for i in range(pp)]
    vc_refs = [bufs[1].at[slot, i] for i in range(pp)]
    kd_refs = [bufs[2].at[slot, i] for i in range(pp)]
    vd_refs = [bufs[3].at[slot, i] for i in range(pp)]
    r = lax.broadcasted_iota(jnp.int32, (QROWS, 1), 0)
    tc_col = (r // C_GROUP) % t_len
    td_col = (r // D_GROUP) % t_len
    head_c = (r // (2 * t_len * C_GROUP)) * C_GROUP + r % C_GROUP
    slope = jnp.zeros((QROWS, 1), F32)
    for h in range(C_HEADS):
        slope = jnp.where(head_c == h, _slope(h), slope)
    qc = qc_ref[...] * SCALE
    qd = qd_ref[...] * SCALE

    @pl.when(j == 0)
    def _():
        def per_head(tp, cols):
            return jnp.concatenate(
                [jnp.broadcast_to(ps_ref[tp:tp + 1, cols + hk * 128:cols + (hk + 1) * 128], (hrows, 128))
                 for hk in range(C_KV)], axis=0)

        scs = []
        for tp in range(t_len):
            sc = jnp.sum(qc * per_head(tp, 1024), axis=-1, keepdims=True)
            sc = sc - slope * (tc_col - tp).astype(F32)
            scs.append(jnp.where(tp <= tc_col, sc, NEG))
        m = scs[0]
        for sc in scs[1:]:
            m = jnp.maximum(m, sc)
        l = jnp.zeros((QROWS, 1), F32)
        acc = jnp.zeros((QROWS, 128), F32)
        for tp in range(t_len):
            p = jnp.exp(scs[tp] - m)
            l = l + p
            acc = acc + p * per_head(tp, 1280)
        m_ref[...] = m
        l_ref[...] = l
        accc_ref[...] = acc
        c = jnp.zeros((QROWS, 1), F32)
        acc = jnp.zeros((QROWS, 256), F32)
        for tp in range(t_len - 1, -1, -1):
            z = jnp.sum(qd * ps_ref[tp:tp + 1, 2560:2816], axis=-1, keepdims=True)
            earlier = tp < td_col
            sp = jnp.where(earlier, _softplus(z), 0.0)
            a = jnp.exp(jnp.where(earlier, z - sp - c, NEG))
            acc = acc + a * ps_ref[tp:tp + 1, 2816:3072]
            c = c + sp
        c_ref[...] = c
        accd_ref[...] = acc

    qcb = qc.astype(BF16)
    qpos = (past_len + tc_col).astype(F32)
    col = lax.broadcasted_iota(jnp.int32, (1, PAGE * C_KV), 1)
    own_head = (col % C_KV) == (r // hrows)
    ckey = col // C_KV
    ss = []
    for i in range(pp):
        page = npages - 1 - (j * pp + i)
        kpos = (page * PAGE + ckey).astype(F32)
        sc = _nt(qcb, kc_refs[i][...].astype(BF16))
        ss.append(jnp.where(own_head, sc - slope * (qpos - kpos), NEG))
    s = jnp.concatenate(ss, axis=1)
    m_old = m_ref[...]
    m_new = jnp.maximum(m_old, jnp.max(s, axis=-1, keepdims=True))
    alpha = jnp.exp(m_old - m_new)
    p = jnp.exp(s - m_new)
    l_ref[...] = alpha * l_ref[...] + jnp.sum(p, axis=-1, keepdims=True)
    p = p.astype(BF16)
    w = PAGE * C_KV
    pv = _mm(p[:, 0:w], vc_refs[0][...].astype(BF16))
    for i in range(1, pp):
        pv = pv + _mm(p[:, i * w:(i + 1) * w], vc_refs[i][...].astype(BF16))
    accc_ref[...] = alpha * accc_ref[...] + pv
    m_ref[...] = m_new

    qdb = qd.astype(BF16)
    kd_all = jnp.concatenate([kd_refs[i][...].astype(BF16) for i in range(pp)], axis=1)
    z = _mm(qdb, kd_all)
    sp = _softplus(z)
    hi = sp.astype(BF16)
    lo = (sp - hi.astype(F32)).astype(BF16)
    stacked = jnp.concatenate(
        [jnp.concatenate([hi[:, i * PAGE:(i + 1) * PAGE], lo[:, i * PAGE:(i + 1) * PAGE]], axis=1)
         for i in range(pp)], axis=0)
    tl_all = _mm(stacked, uu_ref[...])
    c = c_ref[...]
    acc = accd_ref[...]
    for i in range(pp):
        tl = tl_all[i * QROWS:(i + 1) * QROWS]
        a = jnp.exp(z[:, i * PAGE:(i + 1) * PAGE] - tl - c)
        acc = acc + _nt(a.astype(BF16), vd_refs[i][...].astype(BF16))
        c = c + tl[:, 0:1]
    c_ref[...] = c
    accd_ref[...] = acc

    @pl.when(j == pl.num_programs(1) - 1)
    def _():
        lam = _lambda(lamv_ref[...], lambda_init)
        o = accc_ref[...] / l_ref[...]
        half = t_len * C_GROUP
        for hk in range(C_KV):
            o0 = o[(hk * 2) * half:(hk * 2 + 1) * half, :]
            o1 = o[(hk * 2 + 1) * half:(hk * 2 + 2) * half, :]
            oc_ref[hk * half:(hk + 1) * half, :] = _sub_rms(o0, o1, lam, g_ref[...], lambda_init)
        per = t_len * D_GROUP
        accd = accd_ref[...]
        for hk in range(D_KV):
            od_ref[hk * per:(hk + 1) * per, :] = accd[hk * per:(hk + 1) * per, hk * HD:(hk + 1) * HD]


def _sample_odd(ps, pools, page_base, page_table, lamv, subln_g, lambda_init, t_len=4):
    n, npages = page_table.shape
    pp = PAGES_PER_STEP
    ps3 = ps.reshape(n, t_len, ps.shape[1])
    eye2 = jnp.eye(2, dtype=F32)
    eye4 = jnp.eye(4, dtype=F32)
    qc = ps3[:, :, 0:1024].reshape(n, t_len, C_KV, C_GROUP, 2, HD).transpose(0, 2, 4, 1, 3, 5)
    qc = (qc[:, :, :, :, :, None, :] * eye2[None, None, :, None, None, :, None]).reshape(n, QROWS, 128)
    qd = ps3[:, :, 1536:2560].reshape(n, t_len, D_KV, D_GROUP, HD).transpose(0, 2, 1, 3, 4)
    qd = (qd[:, :, :, :, None, :] * eye4[None, :, None, None, :, None]).reshape(n, QROWS, 256)
    pt = page_table + page_base

    nsteps = npages // pp
    assert nsteps % 2 == 0, "the page double buffer keys its slot on the step parity"
    per_b = lambda w, rws: pl.BlockSpec((None, rws, w), lambda b, j, pt_ref: (b, 0, 0))
    fixed = lambda shape: pl.BlockSpec(shape, lambda b, j, pt_ref: (0, 0))
    in_specs = [per_b(128, QROWS), per_b(256, QROWS), per_b(ps.shape[1], t_len)]
    in_specs += [pl.BlockSpec(memory_space=pl.ANY)] * 4
    in_specs += [fixed((2 * PAGE, PAGE)), fixed((4, HD)), fixed((1, 2 * HD))]
    ins = [qc, qd, ps3, *pools, _tri_matrix(PAGE), lamv, subln_g.reshape(1, 2 * HD)]
    page_buf = pltpu.VMEM((2, pp, 256, PAGE), F32)
    oc, od = pl.pallas_call(
        functools.partial(_sample_odd_kernel, npages=npages, t_len=t_len, past_len=npages * PAGE,
                          lambda_init=lambda_init),
        grid_spec=pltpu.PrefetchScalarGridSpec(
            num_scalar_prefetch=1,
            grid=(n, nsteps),
            in_specs=in_specs,
            out_specs=[per_b(2 * HD, C_KV * t_len * C_GROUP), per_b(HD, D_KV * t_len * D_GROUP)],
            scratch_shapes=[page_buf, page_buf, page_buf, page_buf, pltpu.SemaphoreType.DMA((2,)),
                            pltpu.VMEM((QROWS, 1), F32), pltpu.VMEM((QROWS, 1), F32), pltpu.VMEM((QROWS, 128), F32),
                            pltpu.VMEM((QROWS, 1), F32), pltpu.VMEM((QROWS, 256), F32)]),
        out_shape=[jax.ShapeDtypeStruct((n, C_KV * t_len * C_GROUP, 2 * HD), F32),
                   jax.ShapeDtypeStruct((n, D_KV * t_len * D_GROUP, HD), F32)],
        compiler_params=_params(("arbitrary", "arbitrary")),
        name="sample_odd_attn",
    )(pt, *ins)
    oc = oc.reshape(n, C_KV, t_len, C_GROUP, 2 * HD).transpose(0, 2, 1, 3, 4).reshape(n * t_len, C_HEADS * 2 * HD)
    od = od.reshape(n, D_KV, t_len, D_GROUP, HD).transpose(0, 2, 1, 3, 4).reshape(n * t_len, D_HEADS * HD)
    return oc, od


def kernel(x_prompt, x_sample, cache_a1, cache_a2, cache_a3, cache_b, cache_c_k, cache_c_v, cache_d_k, cache_d_v,
           state_conv, page_table, w_in_even, w_out_even, sinks_b, w_in_odd, w_out_odd,
           lam_q1, lam_k1, lam_q2, lam_k2, subln_g, w_ffn_a, conv_w, conv_b, w_ffn_g, w_ffn_down,
           ln_mix_g, ln_mix_b, ln_ffn_g, ln_ffn_b):
    bsz, seq, d = x_prompt.shape
    nb, t_len, _ = x_sample.shape
    assert bsz == 1, "the prompt group is one sequence"
    xp = x_prompt.reshape(seq, d)
    xs = x_sample.reshape(nb * t_len, d)
    b_off = [(h // (B_HEADS // B_KV)) * HD for h in range(B_HEADS)]
    n_phys = cache_c_k.shape[1]
    pools = [c.reshape(c.shape[0] * n_phys, PAGE * C_KV, 2 * HD) for c in (cache_c_k, cache_c_v)]
    pools += [jnp.transpose(c, (0, 1, 3, 4, 2)).reshape(c.shape[0] * n_phys, D_KV * HD, PAGE)
              for c in (cache_d_k, cache_d_v)]
    even_p, even_s, odd_p, odd_s, conv_p, conv_s = [], [], [], [], [], []
    for layer in range(DEPTH):
        i = layer // 2
        if layer % 2 == 0:
            w_in = jnp.pad(w_in_even[i].astype(BF16), ((0, 0), (0, EVEN_PAD - EVEN_IN)))
            w_out = w_out_even[i].astype(BF16)
            pp = _matmul(xp, w_in, 512, 1408)
            ps = _matmul(xs, w_in, 512, 1408)
            prev = None
            for g, (_, dil) in enumerate(A_PATTERNS):
                last = g == len(A_PATTERNS) - 1
                res = _band_prompt(pp, dil, 4 if dil == 1 else 1, g * 512, 1536 + g * 512, 3072 + g * 512,
                                   2, 128, (0, HD), head_step=0.25, prev=prev, want_lse=not last)
                prev = None if last else res
            oa_p = res
            ob_p = _band_prompt(pp, 1, 4, 4608, 5120, 5248, B_HEADS, 128, b_off, sinks=sinks_b[i].reshape(B_HEADS))
            oa_s, ob_s = _sample_even(ps, cache_a1, cache_a2, cache_a3, cache_b, i, sinks_b[i].reshape(B_HEADS), t_len)
            w1, w2 = w_out[:512], w_out[512:]
            rows_p, rows_s = [], []
            for g, (win, _) in enumerate(A_PATTERNS):
                w = min(win, seq)
                kv = jnp.stack([pp[seq - w:, 1536 + g * 512:2048 + g * 512].reshape(w, A_HEADS, HD),
                                pp[seq - w:, 3072 + g * 512:3584 + g * 512].reshape(w, A_HEADS, HD)], axis=1)
                rows_p.append(kv[None])
                rows_s.append(jnp.stack([ps[:, 1536 + g * 512:2048 + g * 512].reshape(nb, t_len, A_HEADS, HD),
                                         ps[:, 3072 + g * 512:3584 + g * 512].reshape(nb, t_len, A_HEADS, HD)], axis=2))
            w = min(128, seq)
            rows_p.append(jnp.stack([pp[seq - w:, 5120:5248].reshape(w, B_KV, HD),
                                     pp[seq - w:, 5248:5376].reshape(w, B_KV, HD)], axis=1)[None])
            rows_s.append(jnp.stack([ps[:, 5120:5248].reshape(nb, t_len, B_KV, HD),
                                     ps[:, 5248:5376].reshape(nb, t_len, B_KV, HD)], axis=2))
            even_p.append(rows_p)
            even_s.append(rows_s)
            mix_p, mix_s = (oa_p, ob_p), (oa_s, ob_s)
        else:
            lambda_init = 0.8 - 0.6 * math.exp(-0.3 * layer)
            lamv = jnp.stack([lam_q1[i], lam_k1[i], lam_q2[i], lam_k2[i]])
            w_in = w_in_odd[i].astype(BF16)
            w_out = w_out_odd[i].astype(BF16)
            pp = _matmul(xp, w_in, 512, 1024)
            ps = _matmul(xs, w_in, 512, 1024)
            wvt = jnp.concatenate([w_in[:, 1280:1536], w_in[:, 2816:3072]], axis=1).T
            vt = _matmul_t(wvt, xp)
            oc_p = _diff_prompt(pp, vt, lamv, subln_g[i], lambda_init)
            od_p = _sb_prompt(pp, vt)
            oc_s, od_s = _sample_odd(ps, pools, i * n_phys, page_table, lamv, subln_g[i], lambda_init, t_len)
            w1, w2 = w_out[:1024], w_out[1024:]
            odd_p.append((pp[:, 1024:1280].reshape(1, seq, C_KV, 2 * HD), pp[:, 1280:1536].reshape(1, seq, C_KV, 2 * HD),
                          pp[:, 2560:2816].reshape(1, seq, D_KV, HD), pp[:, 2816:3072].reshape(1, seq, D_KV, HD)))
            odd_s.append((ps[:, 1024:1280].reshape(nb, t_len, C_KV, 2 * HD), ps[:, 1280:1536].reshape(nb, t_len, C_KV, 2 * HD),
                          ps[:, 2560:2816].reshape(nb, t_len, D_KV, HD), ps[:, 2816:3072].reshape(nb, t_len, D_KV, HD)))
            mix_p, mix_s = (oc_p, od_p), (oc_s, od_s)
        xp = _outproj_ln(mix_p[0], mix_p[1], w1, w2, xp, ln_mix_g[layer], ln_mix_b[layer])
        xs = _outproj_ln(mix_s[0], mix_s[1], w1, w2, xs, ln_mix_g[layer], ln_mix_b[layer])
        wa, wg, wd = w_ffn_a[layer].astype(BF16), w_ffn_g[layer].astype(BF16), w_ffn_down[layer].astype(BF16)
        ffn_args = (wa, wg, wd, conv_w[layer], conv_b[layer], ln_ffn_g[layer], ln_ffn_b[layer])
        xp, tail_p = _ffn(xp, *ffn_args)
        st = state_conv[layer]
        p1 = jnp.concatenate([st[:, 1:2], jnp.zeros((nb, t_len - 1, D_FF), F32)], axis=1).reshape(nb * t_len, D_FF)
        p2 = jnp.concatenate([st, jnp.zeros((nb, t_len - 2, D_FF), F32)], axis=1).reshape(nb * t_len, D_FF)
        xs, tail_s = _ffn(xs, *ffn_args, prev=(p1, p2), t_len=t_len)
        conv_p.append(tail_p[-2:][None])
        conv_s.append(tail_s.reshape(nb, t_len, D_FF)[:, t_len - 2:])
    stack = lambda per_layer, j: jnp.stack([entry[j] for entry in per_layer])
    outs = [xp.reshape(1, seq, d), xs.reshape(nb, t_len, d)]
    for j in range(4):
        outs += [stack(even_p, j), stack(even_s, j)]
    for j in range(4):
        outs += [stack(odd_p, j), stack(odd_s, j)]
    outs += [jnp.stack(conv_p), jnp.stack(conv_s)]
    return tuple(outs)
```

```python
import functools
import math

import numpy as np
import jax
import jax.numpy as jnp
from jax import lax
from jax.experimental import pallas as pl
from jax.experimental.pallas import tpu as pltpu

F32 = jnp.float32
BF16 = jnp.bfloat16

D_MODEL = 2048
DEPTH = 2
PAGE = 128
HD = 64
BLK = 128
A_PATTERNS = ((128, 1), (512, 4), (2048, 16))
A_HEADS = 8
B_HEADS = 8
B_KV = 2
C_HEADS = 8
C_KV = 2
D_HEADS = 16
D_KV = 4
D_FF = 5632
LN_EPS = 1e-5
NEG = -1e30
ALPHA = (2 * DEPTH) ** 0.25
EVEN_IN = 5376
EVEN_PAD = 5632
ODD_IN = 3072
SCALE = HD ** -0.5

VMEM_LIMIT = 56 * 1024 * 1024


def _params(sem):
    return pltpu.CompilerParams(dimension_semantics=sem, vmem_limit_bytes=VMEM_LIMIT)


def _nt(a, b):
    return lax.dot_general(a, b, (((1,), (1,)), ((), ())), preferred_element_type=F32)


def _mm(a, b):
    return jnp.dot(a, b, preferred_element_type=F32)


def _slope(h, n=8):
    return 2.0 ** (-8.0 * (h + 1) / n)


def _matmul_kernel(x_ref, w_ref, o_ref, xb_ref):
    @pl.when(pl.program_id(1) == 0)
    def _():
        xb_ref[...] = x_ref[...].astype(BF16)

    o_ref[...] = _mm(xb_ref[...], w_ref[...])


def _matmul(x, w, tm, tn):
    m, k = x.shape
    n = w.shape[1]
    tm = min(tm, m)
    return pl.pallas_call(
        _matmul_kernel,
        grid=(m // tm, n // tn),
        in_specs=[pl.BlockSpec((tm, k), lambda i, j: (i, 0)),
                  pl.BlockSpec((k, tn), lambda i, j: (0, j))],
        out_specs=pl.BlockSpec((tm, tn), lambda i, j: (i, j)),
        out_shape=jax.ShapeDtypeStruct((m, n), F32),
        scratch_shapes=[pltpu.VMEM((tm, k), BF16)],
        compiler_params=_params(("parallel", "arbitrary")),
        name="proj_matmul",
    )(x, w)


def _layer_norm_rows(r, g, b):
    mu = jnp.mean(r, axis=-1, keepdims=True)
    d = r - mu
    var = jnp.mean(d * d, axis=-1, keepdims=True)
    return d * lax.rsqrt(var + LN_EPS) * g + b


def _outproj_ln_kernel(m1_ref, m2_ref, w1_ref, w2_ref, x_ref, g_ref, b_ref, o_ref):
    y = _mm(m1_ref[...].astype(BF16), w1_ref[...]) + _mm(m2_ref[...].astype(BF16), w2_ref[...])
    o_ref[...] = _layer_norm_rows(ALPHA * x_ref[...] + y, g_ref[...], b_ref[...])


def _outproj_ln(m1, m2, w1, w2, x, g, b, tm=256):
    m, d = x.shape
    k1, k2 = m1.shape[1], m2.shape[1]
    tm = min(tm, m)
    row = lambda i: (i, 0)
    fixed = lambda i: (0, 0)
    return pl.pallas_call(
        _outproj_ln_kernel,
        grid=(m // tm,),
        in_specs=[pl.BlockSpec((tm, k1), row), pl.BlockSpec((tm, k2), row),
                  pl.BlockSpec((k1, d), fixed), pl.BlockSpec((k2, d), fixed),
                  pl.BlockSpec((tm, d), row), pl.BlockSpec((1, d), fixed), pl.BlockSpec((1, d), fixed)],
        out_specs=pl.BlockSpec((tm, d), row),
        out_shape=jax.ShapeDtypeStruct((m, d), F32),
        compiler_params=_params(("parallel",)),
        name="outproj_ln",
    )(m1, m2, w1, w2, x, g.reshape(1, d), b.reshape(1, d))


FFN_HALO = 16


def _ffn_kernel(*refs, tm, sample, t_len):
    if sample:
        x_ref, p1_ref, p2_ref, wa_ref, wg_ref, wd_ref, cw_ref, cb_ref, g_ref, b_ref, o_ref, tail_ref, xb_ref, acc_ref = refs
    else:
        x_ref, xh_ref, wa_ref, wg_ref, wd_ref, cw_ref, cb_ref, g_ref, b_ref, o_ref, tail_ref, xb_ref, acc_ref = refs
    i = pl.program_id(0)
    f = pl.program_id(1)

    @pl.when(f == 0)
    def _():
        xb_ref[FFN_HALO:, :] = x_ref[...].astype(BF16)
        if sample:
            xb_ref[:FFN_HALO, :] = jnp.zeros((FFN_HALO, x_ref.shape[1]), BF16)
        else:
            halo = jnp.where(i > 0, xh_ref[...], 0.0)
            xb_ref[:FFN_HALO, :] = halo.astype(BF16)
        acc_ref[...] = jnp.zeros_like(acc_ref)

    a_ext = _mm(xb_ref[...], wa_ref[...])
    gate = _mm(xb_ref[FFN_HALO:, :], wg_ref[...])
    a = a_ext[FFN_HALO:, :]
    a1 = pltpu.roll(a_ext, 1, 0)[FFN_HALO:, :]
    a2 = pltpu.roll(a_ext, 2, 0)[FFN_HALO:, :]
    if sample:
        t = lax.broadcasted_iota(jnp.int32, a.shape, 0) % t_len
        a1 = jnp.where(t == 0, p1_ref[...], a1)
        a2 = jnp.where(t < 2, p2_ref[...], a2)
    cw = cw_ref[...]
    c = a2 * cw[0:1, :] + a1 * cw[1:2, :] + a * cw[2:3, :] + cb_ref[...]
    h = (c * jax.nn.sigmoid(c)) * gate
    acc_ref[...] += _mm(h.astype(BF16), wd_ref[...])
    tail_ref[...] = a[tm - tail_ref.shape[0]:, :]

    @pl.when(f == pl.num_programs(1) - 1)
    def _():
        o_ref[...] = _layer_norm_rows(ALPHA * x_ref[...] + acc_ref[...], g_ref[...], b_ref[...])


def _ffn(x, wa, wg, wd, cw, cb, g, b, prev=None, t_len=4, tm=512, tf=512):
    m, d = x.shape
    dff = wa.shape[1]
    tm = min(tm, m)
    sample = prev is not None
    row = lambda i, f: (i, 0)
    col = lambda i, f: (0, f)
    wcol = pl.BlockSpec((d, tf), col)
    ins = [x]
    specs = [pl.BlockSpec((tm, d), row)]
    if sample:
        ins += list(prev)
        specs += [pl.BlockSpec((tm, tf), lambda i, f: (i, f))] * 2
        tail_rows = tm
        tail_spec = pl.BlockSpec((tm, tf), lambda i, f: (i, f))
        tail_shape = (m, dff)
    else:
        ins += [x]
        hb = tm // FFN_HALO
        specs += [pl.BlockSpec((FFN_HALO, d), lambda i, f: (jnp.maximum(i * hb - 1, 0), 0))]
        tail_rows = 8
        tail_spec = pl.BlockSpec((8, tf), lambda i, f: (i, f))
        tail_shape = (8 * (m // tm), dff)
    ins += [wa, wg, wd, cw, cb.reshape(1, dff), g.reshape(1, d), b.reshape(1, d)]
    specs += [wcol, wcol, pl.BlockSpec((tf, d), lambda i, f: (f, 0)),
              pl.BlockSpec((3, tf), col), pl.BlockSpec((1, tf), col),
              pl.BlockSpec((1, d), lambda i, f: (0, 0)), pl.BlockSpec((1, d), lambda i, f: (0, 0))]
    return pl.pallas_call(
        functools.partial(_ffn_kernel, tm=tm, sample=sample, t_len=t_len),
        grid=(m // tm, dff // tf),
        in_specs=specs,
        out_specs=[pl.BlockSpec((tm, d), row), tail_spec],
        out_shape=[jax.ShapeDtypeStruct((m, d), F32), jax.ShapeDtypeStruct(tail_shape, F32)],
        scratch_shapes=[pltpu.VMEM((tm + FFN_HALO, d), BF16), pltpu.VMEM((tm, d), F32)],
        compiler_params=_params(("arbitrary", "arbitrary")),
        name="conv_ffn_sample" if sample else "conv_ffn_prompt",
    )(*ins)


def _band_prompt_kernel(*refs, dil, sub, kv_off, head_step, has_sink, has_prev, want_lse):
    it = iter(refs)
    q_ref, kc_ref, kp_ref, vc_ref, vp_ref = (next(it) for _ in range(5))
    sink_ref = next(it) if has_sink else None
    op_ref, lp_ref = (next(it), next(it)) if has_prev else (None, None)
    o_ref = next(it)
    l_ref = next(it) if want_lse else None
    c = pl.program_id(0)
    hg = pl.program_id(1)
    nh = len(kv_off)
    group_scale = jnp.float32(1.0)
    for n in range(1, 4):
        group_scale = jnp.where(hg == n, jnp.float32(head_step ** n), group_scale)
    qi = lax.broadcasted_iota(jnp.int32, (BLK, BLK), 0)
    kj = lax.broadcasted_iota(jnp.int32, (BLK, BLK), 1)
    mask_c = kj <= qi
    mask_p = kj >= qi
    dist_c = (qi - kj).astype(F32) * float(dil)
    dist_p = (BLK + qi - kj).astype(F32) * float(dil)

    def rows(start):
        return pl.ds(start, BLK, stride=dil) if dil > 1 else pl.ds(start, BLK)

    for res in range(dil):
        for w in range(sub):
            cur = rows(res + BLK * w * dil)
            q_all = q_ref[cur, :]
            kc_all, vc_all = kc_ref[cur, :], vc_ref[cur, :]
            if w == 0:
                kp_all, vp_all = kp_ref[rows(res), :], vp_ref[rows(res), :]
                mp = jnp.logical_and(mask_p, c > 0)
            else:
                prv = rows(res + BLK * (w - 1) * dil)
                kp_all, vp_all = kc_ref[prv, :], vc_ref[prv, :]
                mp = mask_p
            if has_prev:
                op_all, lp_all = op_ref[cur, :], lp_ref[cur, :]
            o_parts, l_parts = [], []
            for h in range(nh):
                hs = slice(h * HD, (h + 1) * HD)
                ks = slice(kv_off[h], kv_off[h] + HD)
                slope = _slope(h) * group_scale
                q = q_all[:, hs].astype(BF16)
                s_c = _nt(q, kc_all[:, ks].astype(BF16)) * SCALE - slope * dist_c
                s_p = _nt(q, kp_all[:, ks].astype(BF16)) * SCALE - slope * dist_p
                s_c = jnp.where(mask_c, s_c, NEG)
                s_p = jnp.where(mp, s_p, NEG)
                m = jnp.maximum(jnp.max(s_c, axis=-1, keepdims=True), jnp.max(s_p, axis=-1, keepdims=True))
                p_c = jnp.exp(s_c - m)
                p_p = jnp.exp(s_p - m)
                l = jnp.sum(p_c, axis=-1, keepdims=True) + jnp.sum(p_p, axis=-1, keepdims=True)
                if has_sink:
                    l = l + jnp.exp(sink_ref[h] - m)
                o = (_mm(p_c.astype(BF16), vc_all[:, ks].astype(BF16))
                     + _mm(p_p.astype(BF16), vp_all[:, ks].astype(BF16)))
                o = o / l
                lse = m + jnp.log(l)
                if has_prev:
                    lse_prev = lp_all[:, hs]
                    lse_new = jnp.logaddexp(lse_prev, lse)
                    o = op_all[:, hs] * jnp.exp(lse_prev - lse_new) + o * jnp.exp(lse - lse_new)
                    lse = lse_new
                o_parts.append(o)
                l_parts.append(jnp.broadcast_to(lse, (BLK, HD)))
            o_ref[cur, :] = jnp.concatenate(o_parts, axis=1)
            if want_lse:
                l_ref[cur, :] = jnp.concatenate(l_parts, axis=1)


def _band_prompt(proj, dil, sub, q_col, k_col, v_col, nh, kv_w, kv_off, head_step=1.0, sinks=None, prev=None,
                 want_lse=False):
    s, w = proj.shape
    chunk = BLK * dil * sub
    tail = BLK * dil
    qw = nh * HD
    ngroups = 512 // qw
    kv_groups = 1 if kv_w == 128 and nh == 8 else ngroups
    kcol = (lambda g: k_col // kv_w + g) if kv_groups > 1 else (lambda g: k_col // kv_w)
    vcol = (lambda g: v_col // kv_w + g) if kv_groups > 1 else (lambda g: v_col // kv_w)
    qspec = pl.BlockSpec((chunk, qw), lambda c, g: (c, q_col // qw + g))
    kc = pl.BlockSpec((chunk, kv_w), lambda c, g: (c, kcol(g)))
    kp = pl.BlockSpec((tail, kv_w), lambda c, g: (jnp.maximum(c * sub - 1, 0), kcol(g)))
    vc = pl.BlockSpec((chunk, kv_w), lambda c, g: (c, vcol(g)))
    vp = pl.BlockSpec((tail, kv_w), lambda c, g: (jnp.maximum(c * sub - 1, 0), vcol(g)))
    ospec = pl.BlockSpec((chunk, qw), lambda c, g: (c, g))
    ins = [proj, proj, proj, proj, proj]
    specs = [qspec, kc, kp, vc, vp]
    if sinks is not None:
        ins.append(sinks)
        specs.append(pl.BlockSpec(memory_space=pltpu.SMEM))
    if prev is not None:
        ins += [prev[0], prev[1]]
        specs += [ospec, ospec]
    oshape = jax.ShapeDtypeStruct((s, 512), F32)
    return pl.pallas_call(
        functools.partial(_band_prompt_kernel, dil=dil, sub=sub, kv_off=tuple(kv_off), head_step=head_step,
                          has_sink=sinks is not None, has_prev=prev is not None, want_lse=want_lse),
        grid=(s // chunk, ngroups),
        in_specs=specs,
        out_specs=[ospec, ospec] if want_lse else ospec,
        out_shape=[oshape, oshape] if want_lse else oshape,
        compiler_params=_params(("parallel", "parallel")),
        name=f"band_prompt_d{dil}",
    )(*ins)


def _col_consts(values, rows):
    r = lax.broadcasted_iota(jnp.int32, (rows, 1), 0)
    col = jnp.zeros((rows, 1), F32)
    for h, v in enumerate(values):
        col = jnp.where(r == h, v, col)
    return col


def _sample_even_kernel(ps_ref, a1_ref, a2_ref, a3_ref, cb_ref, fold_ref, sink_ref, oa_ref, ob_ref, *, t_len):
    nh = 8
    nr = t_len * nh
    row = lax.broadcasted_iota(jnp.int32, (nr, 1), 0)
    t_col = row // nh
    h_col = row % nh
    hm = (lax.broadcasted_iota(jnp.int32, (nr, nh * HD), 1) // HD == h_col).astype(F32)
    slope = jnp.zeros((nr, 1), F32)
    sink_col = jnp.zeros((nr, 1), F32)
    for h in range(nh):
        slope = jnp.where(h_col == h, _slope(h), slope)
        sink_col = jnp.where(h_col == h, sink_ref[h], sink_col)

    def q_rows(col):
        return jnp.concatenate([jnp.broadcast_to(ps_ref[t:t + 1, col:col + 512], (nh, 512))
                                for t in range(t_len)], axis=0) * hm

    def attend(qk, q_new, kt_ref_slice, vt_ref_slice, k_col, v_col, width, dil, sinks):
        kt = kt_ref_slice.astype(BF16)
        length = kt.shape[1]
        dd = length + t_col - lax.broadcasted_iota(jnp.int32, (nr, length), 1)
        valid = jnp.logical_and((dd & (dil - 1)) == 0, dd <= dil * BLK)
        s = jnp.where(valid, _mm(qk, kt) * SCALE - slope * dd.astype(F32), NEG)
        m = jnp.max(s, axis=-1, keepdims=True)
        new = []
        for tp in range(t_len):
            dn = t_col - tp
            ok = jnp.logical_and(dn >= 0, (dn & (dil - 1)) == 0)
            sc = jnp.sum(q_new * ps_ref[tp:tp + 1, k_col:k_col + width], axis=-1, keepdims=True) * SCALE
            sc = jnp.where(ok, sc - slope * dn.astype(F32), NEG)
            new.append(sc)
            m = jnp.maximum(m, sc)
        p = jnp.exp(s - m)
        l = jnp.sum(p, axis=-1, keepdims=True)
        o = _nt(p.astype(BF16), vt_ref_slice.astype(BF16))
        for tp in range(t_len):
            pn = jnp.exp(new[tp] - m)
            l = l + pn
            o = o + pn * ps_ref[tp:tp + 1, v_col:v_col + width]
        if sinks:
            l = l + jnp.exp(sink_col - m)
        return o / l, m + jnp.log(l)

    def head_rows(o_wide):
        return jnp.sum((o_wide * hm).reshape(t_len, nh, nh * HD), axis=1)

    outs, lses = [], []
    for g, (buf, (_, dil)) in enumerate(zip((a1_ref, a2_ref, a3_ref), A_PATTERNS)):
        q = q_rows(g * 512)
        o, lse = attend(q.astype(BF16), q, buf[0:512, :], buf[512:1024, :],
                        1536 + g * 512, 3072 + g * 512, 512, dil, False)
        outs.append(o)
        lses.append(lse)
    mx = jnp.maximum(jnp.maximum(lses[0], lses[1]), lses[2])
    es = [jnp.exp(l_ - mx) for l_ in lses]
    merged = (es[0] * outs[0] + es[1] * outs[1] + es[2] * outs[2]) / (es[0] + es[1] + es[2])
    oa_ref[...] = head_rows(merged)

    qf = _mm(q_rows(4608).astype(BF16), fold_ref[...])
    o, _ = attend(qf.astype(BF16), qf, cb_ref[0:128, :], cb_ref[128:256, :], 5120, 5248, 128, 1, True)
    o_sw = pltpu.roll(o, HD, 1)
    o_sel = jnp.where((h_col // 4) == (h_col % 2), o, o_sw)
    ob_ref[...] = head_rows(jnp.concatenate([o_sel] * 4, axis=1))


def _fold_matrix():
    c = np.arange(512)[:, None]
    l = np.arange(128)[None, :]
    return jnp.asarray(((c % HD == l % HD) & (l // HD == (c // HD) // 4)).astype(np.float32), BF16)


def _sample_even(ps, a1, a2, a3, cb, layer, sinks, t_len=4):
    nl, n = a1.shape[0], a1.shape[1]
    ps3 = ps.reshape(n, t_len, ps.shape[1])

    def rows_last(c):
        return jnp.transpose(c, (0, 1, 3, 4, 5, 2)).reshape(nl * n, -1, c.shape[2])

    a1v, a2v, a3v, cbv = rows_last(a1), rows_last(a2), rows_last(a3), rows_last(cb)
    b3 = lambda arr: pl.BlockSpec((None,) + arr.shape[1:], lambda i: (layer * n + i, 0, 0))
    ospec = pl.BlockSpec((None, t_len, 512), lambda i: (i, 0, 0))
    oa, ob = pl.pallas_call(
        functools.partial(_sample_even_kernel, t_len=t_len),
        grid=(n,),
        in_specs=[pl.BlockSpec((None, t_len, ps.shape[1]), lambda i: (i, 0, 0)),
                  b3(a1v), b3(a2v), b3(a3v), b3(cbv),
                  pl.BlockSpec((512, 128), lambda i: (0, 0)),
                  pl.BlockSpec(memory_space=pltpu.SMEM)],
        out_specs=[ospec, ospec],
        out_shape=[jax.ShapeDtypeStruct((n, t_len, 512), F32)] * 2,
        compiler_params=_params(("parallel",)),
        name="sample_even_attn",
    )(ps3, a1v, a2v, a3v, cbv, _fold_matrix(), sinks)
    return oa.reshape(n * t_len, 512), ob.reshape(n * t_len, 512)


C_GROUP = C_HEADS // C_KV
D_GROUP = D_HEADS // D_KV


def _lambda(lamv, lambda_init):
    s1 = jnp.sum(lamv[0:1, :] * lamv[1:2, :], axis=-1, keepdims=True)
    s2 = jnp.sum(lamv[2:3, :] * lamv[3:4, :], axis=-1, keepdims=True)
    return jnp.exp(s1) - jnp.exp(s2) + lambda_init


def _sub_rms(o0, o1, lam, g, lambda_init):
    d = o0 - lam * o1
    return d * lax.rsqrt(jnp.mean(d * d, axis=-1, keepdims=True) + LN_EPS) * g * (1.0 - lambda_init)


def _softplus(z):
    return jnp.maximum(z, 0.0) + jnp.log(1.0 + jnp.exp2(jnp.abs(z) * (-math.log2(math.e))))


def _tri_matrix(n):
    u = (np.arange(n)[:, None] >= np.arange(n)[None, :]).astype(np.float32)
    return jnp.asarray(np.concatenate([u, u], axis=0), BF16)


def _rev_cumsum(sp, uu):
    hi = sp.astype(BF16)
    lo = (sp - hi.astype(F32)).astype(BF16)
    return _mm(jnp.concatenate([hi, lo], axis=1), uu)


def _matmul_t_kernel(wt_ref, x_ref, o_ref):
    o_ref[...] = _nt(wt_ref[...], x_ref[...].astype(BF16))


def _matmul_t(wt, x, tm=512):
    n, k = wt.shape
    m = x.shape[0]
    tm = min(tm, m)
    return pl.pallas_call(
        _matmul_t_kernel,
        grid=(m // tm,),
        in_specs=[pl.BlockSpec((n, k), lambda i: (0, 0)), pl.BlockSpec((tm, k), lambda i: (i, 0))],
        out_specs=pl.BlockSpec((n, tm), lambda i: (0, i)),
        out_shape=jax.ShapeDtypeStruct((n, m), F32),
        compiler_params=_params(("parallel",)),
        name="proj_matmul_t",
    )(wt, x)


def _diff_prompt_kernel(q_ref, k_ref, vt_ref, lamv_ref, g_ref, o_ref, b0_ref, m_ref, l_ref, acc_ref, *, tq, lambda_init):
    hk = pl.program_id(0)
    i = pl.program_id(1)
    rows = C_GROUP * tq
    lane = lax.broadcasted_iota(jnp.int32, (1, rows), 1)
    slope = jnp.zeros((1, rows), F32)
    for g in range(C_GROUP):
        slope = jnp.where(lane // tq == g, _slope(g), slope)
    slope = slope * jnp.where(hk == 0, 1.0, 2.0 ** (-C_GROUP)).astype(F32)
    kl = lax.broadcasted_iota(jnp.int32, (tq, rows), 0)
    ql = lax.broadcasted_iota(jnp.int32, (tq, rows), 1) % tq
    b0_ref[...] = slope * (kl - ql).astype(F32)
    qs = []
    for m in range(2):
        parts = [q_ref[:, (g * 2 + m) * HD:(g * 2 + m + 1) * HD] for g in range(C_GROUP)]
        qs.append((jnp.concatenate(parts, axis=0) * SCALE).astype(BF16))
    m_ref[...] = jnp.full(m_ref.shape, NEG, F32)
    l_ref[...] = jnp.zeros_like(l_ref)
    acc_ref[...] = jnp.zeros_like(acc_ref)

    def blocks(js, diagonal):
        ks, vts, cjs = [], [], []
        for j in js:
            start = pl.multiple_of(j * tq, tq)
            ks.append(k_ref[pl.ds(start, tq), :].astype(BF16))
            vts.append(vt_ref[:, pl.ds(start, tq)].astype(BF16))
            cjs.append(-slope * ((i - j) * tq).astype(F32))
        for m in range(2):
            ss = []
            m_old = m_ref[m]
            m_new = m_old
            for k, cj in zip(ks, cjs):
                s = _nt(k[:, m * HD:(m + 1) * HD], qs[m]) + b0_ref[...]
                if diagonal:
                    s = jnp.where(kl <= ql, s, NEG)
                ss.append(s)
                m_new = jnp.maximum(m_new, jnp.max(s, axis=0, keepdims=True) + cj)
            alpha = jnp.exp(m_old - m_new)
            l = alpha * l_ref[m]
            acc = alpha * acc_ref[m]
            for s, vt, cj in zip(ss, vts, cjs):
                p = jnp.exp(s + (cj - m_new))
                l = l + jnp.sum(p, axis=0, keepdims=True)
                acc = acc + _mm(vt, p.astype(BF16))
            l_ref[m] = l
            acc_ref[m] = acc
            m_ref[m] = m_new

    @pl.when(i % 2 == 1)
    def _():
        blocks((0,), False)

    def off_diagonal(it, carry):
        j = i % 2 + 2 * it
        blocks((j, j + 1), False)
        return carry

    lax.fori_loop(0, i // 2, off_diagonal, 0)
    blocks((i,), True)
    lam = _lambda(lamv_ref[...], lambda_init)
    d = acc_ref[0] / l_ref[0] - lam * (acc_ref[1] / l_ref[1])
    d = d * lax.rsqrt(jnp.mean(d * d, axis=0, keepdims=True) + LN_EPS) * g_ref[...] * (1.0 - lambda_init)
    for g in range(C_GROUP):
        o_ref[:, g * 2 * HD:(g + 1) * 2 * HD] = d[:, g * tq:(g + 1) * tq].T


def _diff_prompt(proj, vt, lamv, subln_g, lambda_init, tq=256):
    s = proj.shape[0]
    rows = C_GROUP * tq
    return pl.pallas_call(
        functools.partial(_diff_prompt_kernel, tq=tq, lambda_init=lambda_init),
        grid=(C_KV, s // tq),
        in_specs=[pl.BlockSpec((tq, 512), lambda h, i: (i, h)),
                  pl.BlockSpec((s, 128), lambda h, i: (0, 8 + h)),
                  pl.BlockSpec((2 * HD, s), lambda h, i: (h, 0)),
                  pl.BlockSpec((4, HD), lambda h, i: (0, 0)),
                  pl.BlockSpec((2 * HD, 1), lambda h, i: (0, 0))],
        out_specs=pl.BlockSpec((tq, 512), lambda h, i: (i, h)),
        out_shape=jax.ShapeDtypeStruct((s, C_HEADS * 2 * HD), F32),
        scratch_shapes=[pltpu.VMEM((tq, rows), F32), pltpu.VMEM((2, 1, rows), F32), pltpu.VMEM((2, 1, rows), F32),
                        pltpu.VMEM((2, 2 * HD, rows), F32)],
        compiler_params=_params(("parallel", "arbitrary")),
        name="diff_attn_prompt",
    )(proj, proj, vt, lamv, subln_g.reshape(2 * HD, 1))


def _tri_matrix_t(n):
    u = (np.arange(n)[None, :] >= np.arange(n)[:, None]).astype(np.float32)
    return jnp.asarray(np.concatenate([u, u], axis=1), BF16)


def _sb_prompt_kernel(q_ref, k_ref, vt_ref, ut_ref, o_ref, c_ref, acc_ref, *, tq):
    i = pl.program_id(1)
    rows = D_GROUP * tq
    kl = lax.broadcasted_iota(jnp.int32, (tq, rows), 0)
    ql = lax.broadcasted_iota(jnp.int32, (tq, rows), 1) % tq
    qs = []
    for hl in range(2):
        parts = [q_ref[:, (hl * D_GROUP + g) * HD:(hl * D_GROUP + g + 1) * HD] for g in range(D_GROUP)]
        qs.append((jnp.concatenate(parts, axis=0) * SCALE).astype(BF16))
    c_ref[...] = jnp.zeros_like(c_ref)
    acc_ref[...] = jnp.zeros_like(acc_ref)

    def blocks(js, diagonal):
        for hl in range(2):
            c = c_ref[hl]
            pv = None
            for j in js:
                start = pl.multiple_of(j * tq, tq)
                k = k_ref[pl.ds(start, tq), hl * HD:(hl + 1) * HD].astype(BF16)
                vt = vt_ref[hl * HD:(hl + 1) * HD, pl.ds(start, tq)].astype(BF16)
                z = _nt(k, qs[hl])
                sp = _softplus(z)
                if diagonal:
                    sp = jnp.where(kl < ql, sp, 0.0)
                hi = sp.astype(BF16)
                lo = (sp - hi.astype(F32)).astype(BF16)
                tl = _mm(ut_ref[...], jnp.concatenate([hi, lo], axis=0))
                e = z - tl - c
                if diagonal:
                    e = jnp.where(kl < ql, e, NEG)
                term = _mm(vt, jnp.exp(e).astype(BF16))
                pv = term if pv is None else pv + term
                c = c + tl[0:1, :]
            acc_ref[hl * HD:(hl + 1) * HD, :] += pv
            c_ref[hl] = c

    blocks((i,), True)

    @pl.when(i % 2 == 1)
    def _():
        blocks((i - 1,), False)

    top = i - 1 - i % 2

    def older(it, carry):
        blocks((top - 2 * it, top - 2 * it - 1), False)
        return carry

    lax.fori_loop(0, i // 2, older, 0)
    for g in range(D_GROUP):
        t = acc_ref[:, g * tq:(g + 1) * tq].T
        for hl in range(2):
            o_ref[:, (hl * D_GROUP + g) * HD:(hl * D_GROUP + g + 1) * HD] = t[:, hl * HD:(hl + 1) * HD]


def _sb_prompt(proj, vt, tq=256):
    s = proj.shape[0]
    rows = D_GROUP * tq
    return pl.pallas_call(
        functools.partial(_sb_prompt_kernel, tq=tq),
        grid=(D_KV // 2, s // tq),
        in_specs=[pl.BlockSpec((tq, 512), lambda h, i: (i, 3 + h)),
                  pl.BlockSpec((s, 128), lambda h, i: (0, 20 + h)),
                  pl.BlockSpec((2 * HD, s), lambda h, i: (2 + h, 0)),
                  pl.BlockSpec((tq, 2 * tq), lambda h, i: (0, 0))],
        out_specs=pl.BlockSpec((tq, 512), lambda h, i: (i, h)),
        out_shape=jax.ShapeDtypeStruct((s, D_HEADS * HD), F32),
        scratch_shapes=[pltpu.VMEM((2, 1, rows), F32), pltpu.VMEM((2 * HD, rows), F32)],
        compiler_params=_params(("parallel", "arbitrary")),
        name="sb_attn_prompt",
    )(proj, proj, vt, _tri_matrix_t(tq))


PAGES_PER_STEP = 8
QROWS = 64


def _sample_odd_kernel(pt_ref, qc_ref, qd_ref, ps_ref, *refs, npages, t_len, past_len, lambda_init):
    pp = PAGES_PER_STEP
    hrows = QROWS // C_KV
    pools = refs[0:4]
    uu_ref, lamv_ref, g_ref, oc_ref, od_ref = refs[4:9]
    bufs = refs[9:13]
    sem, m_ref, l_ref, accc_ref, c_ref, accd_ref = refs[13:]
    b = pl.program_id(0)
    j = pl.program_id(1)
    nsteps = pl.num_programs(1)
    slot = j % 2

    def page_copies(bb, jj, sl):
        out = []
        for i in range(pp):
            page = pt_ref[bb, npages - 1 - (jj * pp + i)]
            for pool, buf in zip(pools, bufs):
                out.append(pltpu.make_async_copy(pool.at[page], buf.at[sl, i], sem.at[sl]))
        return out

    def start_all(copies):
        for n, cp in enumerate(copies):
            cp.start(priority=n % 2)

    @pl.when(jnp.logical_and(b == 0, j == 0))
    def _():
        start_all(page_copies(0, 0, 0))
        start_all(page_copies(0, 1, 1))

    for cp in page_copies(b, j, slot):
        cp.wait()
    kc_refs = [bufs[0].at[slot, i] for i in range(pp)]
    vc_refs = [bufs[1].at[slot, i] for i in range(pp)]
    kd_refs = [bufs[2].at[slot, i] for i in range(pp)]
    vd_refs = [bufs[3].at[slot, i] for i in range(pp)]
    r = lax.broadcasted_iota(jnp.int32, (QROWS, 1), 0)
    tc_col = (r // C_GROUP) % t_len
    td_col = (r // D_GROUP) % t_len
    head_c = (r // (2 * t_len * C_GROUP)) * C_GROUP + r % C_GROUP
    slope = jnp.zeros((QROWS, 1), F32)
    for h in range(C_HEADS):
        slope = jnp.where(head_c == h, _slope(h), slope)
    qc = qc_ref[...] * SCALE
    qd = qd_ref[...] * SCALE

    @pl.when(j == 0)
    def _():
        def per_head(tp, cols):
            return jnp.concatenate(
                [jnp.broadcast_to(ps_ref[tp:tp + 1, cols + hk * 128:cols + (hk + 1) * 128], (hrows, 128))
                 for hk in range(C_KV)], axis=0)

        scs = []
        for tp in range(t_len):
            sc = jnp.sum(qc * per_head(tp, 1024), axis=-1, keepdims=True)
            sc = sc - slope * (tc_col - tp).astype(F32)
            scs.append(jnp.where(tp <= tc_col, sc, NEG))
        m = scs[0]
        for sc in scs[1:]:
            m = jnp.maximum(m, sc)
        l = jnp.zeros((QROWS, 1), F32)
        acc = jnp.zeros((QROWS, 128), F32)
        for tp in range(t_len):
            p = jnp.exp(scs[tp] - m)
            l = l + p
            acc = acc + p * per_head(tp, 1280)
        m_ref[...] = m
        l_ref[...] = l
        accc_ref[...] = acc
        c = jnp.zeros((QROWS, 1), F32)
        acc = jnp.zeros((QROWS, 256), F32)
        for tp in range(t_len - 1, -1, -1):
            z = jnp.sum(qd * ps_ref[tp:tp + 1, 2560:2816], axis=-1, keepdims=True)
            earlier = tp < td_col
            sp = jnp.where(earlier, _softplus(z), 0.0)
            a = jnp.exp(jnp.where(earlier, z - sp - c, NEG))
            acc = acc + a * ps_ref[tp:tp + 1, 2816:3072]
            c = c + sp
        c_ref[...] = c
        accd_ref[...] = acc

    qcb = qc.astype(BF16)
    qpos = (past_len + tc_col).astype(F32)
    col = lax.broadcasted_iota(jnp.int32, (1, PAGE * C_KV), 1)
    own_head = (col % C_KV) == (r // hrows)
    ckey = col // C_KV
    ss = []
    for i in range(pp):
        page = npages - 1 - (j * pp + i)
        kpos = (page * PAGE + ckey).astype(F32)
        sc = _nt(qcb, kc_refs[i][...].astype(BF16))
        ss.append(jnp.where(own_head, sc - slope * (qpos - kpos), NEG))
    s = jnp.concatenate(ss, axis=1)
    m_old = m_ref[...]
    m_new = jnp.maximum(m_old, jnp.max(s, axis=-1, keepdims=True))
    alpha = jnp.exp(m_old - m_new)
    p = jnp.exp(s - m_new)
    l_ref[...] = alpha * l_ref[...] + jnp.sum(p, axis=-1, keepdims=True)
    p = p.astype(BF16)
    w = PAGE * C_KV
    pv = _mm(p[:, 0:w], vc_refs[0][...].astype(BF16))
    for i in range(1, pp):
        pv = pv + _mm(p[:, i * w:(i + 1) * w], vc_refs[i][...].astype(BF16))
    accc_ref[...] = alpha * accc_ref[...] + pv
    m_ref[...] = m_new

    qdb = qd.astype(BF16)
    kd_all = jnp.concatenate([kd_refs[i][...].astype(BF16) for i in range(pp)], axis=1)
    z = _mm(qdb, kd_all)
    sp = _softplus(z)
    hi = sp.astype(BF16)
    lo = (sp - hi.astype(F32)).astype(BF16)
    stacked = jnp.concatenate(
        [jnp.concatenate([hi[:, i * PAGE:(i + 1) * PAGE], lo[:, i * PAGE:(i + 1) * PAGE]], axis=1)
         for i in range(pp)], axis=0)
    tl_all = _mm(stacked, uu_ref[...])
    c = c_ref[...]
    acc = accd_ref[...]
    for i in range(pp):
        tl = tl_all[i * QROWS:(i + 1) * QROWS]
        a = jnp.exp(z[:, i * PAGE:(i + 1) * PAGE] - tl - c)
        acc = acc + _nt(a.astype(BF16), vd_refs[i][...].astype(BF16))
        c = c + tl[:, 0:1]
    c_ref[...] = c
    accd_ref[...] = acc

    two_ahead = jnp.logical_not(jnp.logical_and(b == pl.num_programs(0) - 1, j >= nsteps - 2))

    @pl.when(two_ahead)
    def _():
        wrap = j + 2 >= nsteps
        start_all(page_copies(jnp.where(wrap, b + 1, b), jnp.where(wrap, j + 2 - nsteps, j + 2), slot))

    @pl.when(j == pl.num_programs(1) - 1)
    def _():
        lam = _lambda(lamv_ref[...], lambda_init)
        o = accc_ref[...] / l_ref[...]
        half = t_len * C_GROUP
        for hk in range(C_KV):
            o0 = o[(hk * 2) * half:(hk * 2 + 1) * half, :]
            o1 = o[(hk * 2 + 1) * half:(hk * 2 + 2) * half, :]
            oc_ref[hk * half:(hk + 1) * half, :] = _sub_rms(o0, o1, lam, g_ref[...], lambda_init)
        per = t_len * D_GROUP
        accd = accd_ref[...]
        for hk in range(D_KV):
            od_ref[hk * per:(hk + 1) * per, :] = accd[hk * per:(hk + 1) * per, hk * HD:(hk + 1) * HD]


def _sample_odd(ps, pools, page_base, page_table, lamv, subln_g, lambda_init, t_len=4):
    n, npages = page_table.shape
    pp = PAGES_PER_STEP
    ps3 = ps.reshape(n, t_len, ps.shape[1])
    eye2 = jnp.eye(2, dtype=F32)
    eye4 = jnp.eye(4, dtype=F32)
    qc = ps3[:, :, 0:1024].reshape(n, t_len, C_KV, C_GROUP, 2, HD).transpose(0, 2, 4, 1, 3, 5)
    qc = (qc[:, :, :, :, :, None, :] * eye2[None, None, :, None, None, :, None]).reshape(n, QROWS, 128)
    qd = ps3[:, :, 1536:2560].reshape(n, t_len, D_KV, D_GROUP, HD).transpose(0, 2, 1, 3, 4)
    qd = (qd[:, :, :, :, None, :] * eye4[None, :, None, None, :, None]).reshape(n, QROWS, 256)
    pt = page_table + page_base

    nsteps = npages // pp
    assert nsteps % 2 == 0, "the page double buffer keys its slot on the step parity"
    per_b = lambda w, rws: pl.BlockSpec((None, rws, w), lambda b, j, pt_ref: (b, 0, 0))
    fixed = lambda shape: pl.BlockSpec(shape, lambda b, j, pt_ref: (0, 0))
    in_specs = [per_b(128, QROWS), per_b(256, QROWS), per_b(ps.shape[1], t_len)]
    in_specs += [pl.BlockSpec(memory_space=pl.ANY)] * 4
    in_specs += [fixed((2 * PAGE, PAGE)), fixed((4, HD)), fixed((1, 2 * HD))]
    ins = [qc, qd, ps3, *pools, _tri_matrix(PAGE), lamv, subln_g.reshape(1, 2 * HD)]
    page_buf = pltpu.VMEM((2, pp, 256, PAGE), F32)
    oc, od = pl.pallas_call(
        functools.partial(_sample_odd_kernel, npages=npages, t_len=t_len, past_len=npages * PAGE,
                          lambda_init=lambda_init),
        grid_spec=pltpu.PrefetchScalarGridSpec(
            num_scalar_prefetch=1,
            grid=(n, nsteps),
            in_specs=in_specs,
            out_specs=[per_b(2 * HD, C_KV * t_len * C_GROUP), per_b(HD, D_KV * t_len * D_GROUP)],
            scratch_shapes=[page_buf, page_buf, page_buf, page_buf, pltpu.SemaphoreType.DMA((2,)),
                            pltpu.VMEM((QROWS, 1), F32), pltpu.VMEM((QROWS, 1), F32), pltpu.VMEM((QROWS, 128), F32),
                            pltpu.VMEM((QROWS, 1), F32), pltpu.VMEM((QROWS, 256), F32)]),
        out_shape=[jax.ShapeDtypeStruct((n, C_KV * t_len * C_GROUP, 2 * HD), F32),
                   jax.ShapeDtypeStruct((n, D_KV * t_len * D_GROUP, HD), F32)],
        compiler_params=_params(("arbitrary", "arbitrary")),
        name="sample_odd_attn",
    )(pt, *ins)
    oc = oc.reshape(n, C_KV, t_len, C_GROUP, 2 * HD).transpose(0, 2, 1, 3, 4).reshape(n * t_len, C_HEADS * 2 * HD)
    od = od.reshape(n, D_KV, t_len, D_GROUP, HD).transpose(0, 2, 1, 3, 4).reshape(n * t_len, D_HEADS * HD)
    return oc, od


def kernel(x_prompt, x_sample, cache_a1, cache_a2, cache_a3, cache_b, cache_c_k, cache_c_v, cache_d_k, cache_d_v,
           state_conv, page_table, w_in_even, w_out_even, sinks_b, w_in_odd, w_out_odd,
           lam_q1, lam_k1, lam_q2, lam_k2, subln_g, w_ffn_a, conv_w, conv_b, w_ffn_g, w_ffn_down,
           ln_mix_g, ln_mix_b, ln_ffn_g, ln_ffn_b):
    bsz, seq, d = x_prompt.shape
    nb, t_len, _ = x_sample.shape
    assert bsz == 1, "the prompt group is one sequence"
    xp = x_prompt.reshape(seq, d)
    xs = x_sample.reshape(nb * t_len, d)
    b_off = [(h // (B_HEADS // B_KV)) * HD for h in range(B_HEADS)]
    n_phys = cache_c_k.shape[1]
    pools = [c.reshape(c.shape[0] * n_phys, PAGE * C_KV, 2 * HD) for c in (cache_c_k, cache_c_v)]
    pools += [jnp.transpose(c, (0, 1, 3, 4, 2)).reshape(c.shape[0] * n_phys, D_KV * HD, PAGE)
              for c in (cache_d_k, cache_d_v)]
    even_p, even_s, odd_p, odd_s, conv_p, conv_s = [], [], [], [], [], []
    for layer in range(DEPTH):
        i = layer // 2
        if layer % 2 == 0:
            w_in = jnp.pad(w_in_even[i].astype(BF16), ((0, 0), (0, EVEN_PAD - EVEN_IN)))
            w_out = w_out_even[i].astype(BF16)
            pp = _matmul(xp, w_in, 512, 1408)
            ps = _matmul(xs, w_in, 512, 1408)
            prev = None
            for g, (_, dil) in enumerate(A_PATTERNS):
                last = g == len(A_PATTERNS) - 1
                res = _band_prompt(pp, dil, 4 if dil == 1 else 1, g * 512, 1536 + g * 512, 3072 + g * 512,
                                   2, 128, (0, HD), head_step=0.25, prev=prev, want_lse=not last)
                prev = None if last else res
            oa_p = res
            ob_p = _band_prompt(pp, 1, 4, 4608, 5120, 5248, B_HEADS, 128, b_off, sinks=sinks_b[i].reshape(B_HEADS))
            oa_s, ob_s = _sample_even(ps, cache_a1, cache_a2, cache_a3, cache_b, i, sinks_b[i].reshape(B_HEADS), t_len)
            w1, w2 = w_out[:512], w_out[512:]
            rows_p, rows_s = [], []
            for g, (win, _) in enumerate(A_PATTERNS):
                w = min(win, seq)
                kv = jnp.stack([pp[seq - w:, 1536 + g * 512:2048 + g * 512].reshape(w, A_HEADS, HD),
                                pp[seq - w:, 3072 + g * 512:3584 + g * 512].reshape(w, A_HEADS, HD)], axis=1)
                rows_p.append(kv[None])
                rows_s.append(jnp.stack([ps[:, 1536 + g * 512:2048 + g * 512].reshape(nb, t_len, A_HEADS, HD),
                                         ps[:, 3072 + g * 512:3584 + g * 512].reshape(nb, t_len, A_HEADS, HD)], axis=2))
            w = min(128, seq)
            rows_p.append(jnp.stack([pp[seq - w:, 5120:5248].reshape(w, B_KV, HD),
                                     pp[seq - w:, 5248:5376].reshape(w, B_KV, HD)], axis=1)[None])
            rows_s.append(jnp.stack([ps[:, 5120:5248].reshape(nb, t_len, B_KV, HD),
                                     ps[:, 5248:5376].reshape(nb, t_len, B_KV, HD)], axis=2))
            even_p.append(rows_p)
            even_s.append(rows_s)
            mix_p, mix_s = (oa_p, ob_p), (oa_s, ob_s)
        else:
            lambda_init = 0.8 - 0.6 * math.exp(-0.3 * layer)
            lamv = jnp.stack([lam_q1[i], lam_k1[i], lam_q2[i], lam_k2[i]])
            w_in = w_in_odd[i].astype(BF16)
            w_out = w_out_odd[i].astype(BF16)
            pp = _matmul(xp, w_in, 512, 1024)
            ps = _matmul(xs, w_in, 512, 1024)
            wvt = jnp.concatenate([w_in[:, 1280:1536], w_in[:, 2816:3072]], axis=1).T
            vt = _matmul_t(wvt, xp)
            oc_p = _diff_prompt(pp, vt, lamv, subln_g[i], lambda_init)
            od_p = _sb_prompt(pp, vt)
            oc_s, od_s = _sample_odd(ps, pools, i * n_phys, page_table, lamv, subln_g[i], lambda_init, t_len)
            w1, w2 = w_out[:1024], w_out[1024:]
            odd_p.append((pp[:, 1024:1280].reshape(1, seq, C_KV, 2 * HD), pp[:, 1280:1536].reshape(1, seq, C_KV, 2 * HD),
                          pp[:, 2560:2816].reshape(1, seq, D_KV, HD), pp[:, 2816:3072].reshape(1, seq, D_KV, HD)))
            odd_s.append((ps[:, 1024:1280].reshape(nb, t_len, C_KV, 2 * HD), ps[:, 1280:1536].reshape(nb, t_len, C_KV, 2 * HD),
                          ps[:, 2560:2816].reshape(nb, t_len, D_KV, HD), ps[:, 2816:3072].reshape(nb, t_len, D_KV, HD)))
            mix_p, mix_s = (oc_p, od_p), (oc_s, od_s)
        xp = _outproj_ln(mix_p[0], mix_p[1], w1, w2, xp, ln_mix_g[layer], ln_mix_b[layer])
        xs = _outproj_ln(mix_s[0], mix_s[1], w1, w2, xs, ln_mix_g[layer], ln_mix_b[layer])
        wa, wg, wd = w_ffn_a[layer].astype(BF16), w_ffn_g[layer].astype(BF16), w_ffn_down[layer].astype(BF16)
        ffn_args = (wa, wg, wd, conv_w[layer], conv_b[layer], ln_ffn_g[layer], ln_ffn_b[layer])
        xp, tail_p = _ffn(xp, *ffn_args)
        st = state_conv[layer]
        p1 = jnp.concatenate([st[:, 1:2], jnp.zeros((nb, t_len - 1, D_FF), F32)], axis=1).reshape(nb * t_len, D_FF)
        p2 = jnp.concatenate([st, jnp.zeros((nb, t_len - 2, D_FF), F32)], axis=1).reshape(nb * t_len, D_FF)
        xs, tail_s = _ffn(xs, *ffn_args, prev=(p1, p2), t_len=t_len)
        conv_p.append(tail_p[-2:][None])
        conv_s.append(tail_s.reshape(nb, t_len, D_FF)[:, t_len - 2:])
    stack = lambda per_layer, j: jnp.stack([entry[j] for entry in per_layer])
    outs = [xp.reshape(1, seq, d), xs.reshape(nb, t_len, d)]
    for j in range(4):
        outs += [stack(even_p, j), stack(even_s, j)]
    for j in range(4):
        outs += [stack(odd_p, j), stack(odd_s, j)]
    outs += [jnp.stack(conv_p), jnp.stack(conv_s)]
    return tuple(outs)
```

```python
import functools
import math

import numpy as np
import jax
import jax.numpy as jnp
from jax import lax
from jax.experimental import pallas as pl
from jax.experimental.pallas import tpu as pltpu

F32 = jnp.float32
BF16 = jnp.bfloat16

D_MODEL = 2048
DEPTH = 2
PAGE = 128
HD = 64
BLK = 128
A_PATTERNS = ((128, 1), (512, 4), (2048, 16))
A_HEADS = 8
B_HEADS = 8
B_KV = 2
C_HEADS = 8
C_KV = 2
D_HEADS = 16
D_KV = 4
D_FF = 5632
LN_EPS = 1e-5
NEG = -1e30
ALPHA = (2 * DEPTH) ** 0.25
EVEN_IN = 5376
EVEN_PAD = 5632
ODD_IN = 3072
SCALE = HD ** -0.5

VMEM_LIMIT = 56 * 1024 * 1024


def _params(sem):
    return pltpu.CompilerParams(dimension_semantics=sem, vmem_limit_bytes=VMEM_LIMIT)


def _nt(a, b):
    return lax.dot_general(a, b, (((1,), (1,)), ((), ())), preferred_element_type=F32)


def _mm(a, b):
    return jnp.dot(a, b, preferred_element_type=F32)


def _slope(h, n=8):
    return 2.0 ** (-8.0 * (h + 1) / n)


def _matmul_kernel(x_ref, w_ref, o_ref, xb_ref):
    @pl.when(pl.program_id(1) == 0)
    def _():
        xb_ref[...] = x_ref[...].astype(BF16)

    o_ref[...] = _mm(xb_ref[...], w_ref[...])


def _matmul(x, w, tm, tn):
    m, k = x.shape
    n = w.shape[1]
    tm = min(tm, m)
    return pl.pallas_call(
        _matmul_kernel,
        grid=(m // tm, n // tn),
        in_specs=[pl.BlockSpec((tm, k), lambda i, j: (i, 0)),
                  pl.BlockSpec((k, tn), lambda i, j: (0, j))],
        out_specs=pl.BlockSpec((tm, tn), lambda i, j: (i, j)),
        out_shape=jax.ShapeDtypeStruct((m, n), F32),
        scratch_shapes=[pltpu.VMEM((tm, k), BF16)],
        compiler_params=_params(("parallel", "arbitrary")),
        name="proj_matmul",
    )(x, w)


def _layer_norm_rows(r, g, b):
    mu = jnp.mean(r, axis=-1, keepdims=True)
    d = r - mu
    var = jnp.mean(d * d, axis=-1, keepdims=True)
    return d * lax.rsqrt(var + LN_EPS) * g + b


def _outproj_ln_kernel(m1_ref, m2_ref, w1_ref, w2_ref, x_ref, g_ref, b_ref, o_ref):
    y = _mm(m1_ref[...].astype(BF16), w1_ref[...]) + _mm(m2_ref[...].astype(BF16), w2_ref[...])
    o_ref[...] = _layer_norm_rows(ALPHA * x_ref[...] + y, g_ref[...], b_ref[...])


def _outproj_ln(m1, m2, w1, w2, x, g, b, tm=256):
    m, d = x.shape
    k1, k2 = m1.shape[1], m2.shape[1]
    tm = min(tm, m)
    row = lambda i: (i, 0)
    fixed = lambda i: (0, 0)
    return pl.pallas_call(
        _outproj_ln_kernel,
        grid=(m // tm,),
        in_specs=[pl.BlockSpec((tm, k1), row), pl.BlockSpec((tm, k2), row),
                  pl.BlockSpec((k1, d), fixed), pl.BlockSpec((k2, d), fixed),
                  pl.BlockSpec((tm, d), row), pl.BlockSpec((1, d), fixed), pl.BlockSpec((1, d), fixed)],
        out_specs=pl.BlockSpec((tm, d), row),
        out_shape=jax.ShapeDtypeStruct((m, d), F32),
        compiler_params=_params(("parallel",)),
        name="outproj_ln",
    )(m1, m2, w1, w2, x, g.reshape(1, d), b.reshape(1, d))


FFN_HALO = 16


def _ffn_kernel(*refs, tm, sample, t_len):
    if sample:
        x_ref, p1_ref, p2_ref, wa_ref, wg_ref, wd_ref, cw_ref, cb_ref, g_ref, b_ref, o_ref, tail_ref, xb_ref, acc_ref = refs
    else:
        x_ref, xh_ref, wa_ref, wg_ref, wd_ref, cw_ref, cb_ref, g_ref, b_ref, o_ref, tail_ref, xb_ref, acc_ref = refs
    i = pl.program_id(0)
    f = pl.program_id(1)

    @pl.when(f == 0)
    def _():
        xb_ref[FFN_HALO:, :] = x_ref[...].astype(BF16)
        if sample:
            xb_ref[:FFN_HALO, :] = jnp.zeros((FFN_HALO, x_ref.shape[1]), BF16)
        else:
            halo = jnp.where(i > 0, xh_ref[...], 0.0)
            xb_ref[:FFN_HALO, :] = halo.astype(BF16)
        acc_ref[...] = jnp.zeros_like(acc_ref)

    a_ext = _mm(xb_ref[...], wa_ref[...])
    gate = _mm(xb_ref[FFN_HALO:, :], wg_ref[...])
    a = a_ext[FFN_HALO:, :]
    a1 = pltpu.roll(a_ext, 1, 0)[FFN_HALO:, :]
    a2 = pltpu.roll(a_ext, 2, 0)[FFN_HALO:, :]
    if sample:
        t = lax.broadcasted_iota(jnp.int32, a.shape, 0) % t_len
        a1 = jnp.where(t == 0, p1_ref[...], a1)
        a2 = jnp.where(t < 2, p2_ref[...], a2)
    cw = cw_ref[...]
    c = a2 * cw[0:1, :] + a1 * cw[1:2, :] + a * cw[2:3, :] + cb_ref[...]
    h = (c * jax.nn.sigmoid(c)) * gate
    acc_ref[...] += _mm(h.astype(BF16), wd_ref[...])
    tail_ref[...] = a[tm - tail_ref.shape[0]:, :]

    @pl.when(f == pl.num_programs(1) - 1)
    def _():
        o_ref[...] = _layer_norm_rows(ALPHA * x_ref[...] + acc_ref[...], g_ref[...], b_ref[...])


def _ffn(x, wa, wg, wd, cw, cb, g, b, prev=None, t_len=4, tm=512, tf=512):
    m, d = x.shape
    dff = wa.shape[1]
    tm = min(tm, m)
    sample = prev is not None
    row = lambda i, f: (i, 0)
    col = lambda i, f: (0, f)
    wcol = pl.BlockSpec((d, tf), col)
    ins = [x]
    specs = [pl.BlockSpec((tm, d), row)]
    if sample:
        ins += list(prev)
        specs += [pl.BlockSpec((tm, tf), lambda i, f: (i, f))] * 2
        tail_rows = tm
        tail_spec = pl.BlockSpec((tm, tf), lambda i, f: (i, f))
        tail_shape = (m, dff)
    else:
        ins += [x]
        hb = tm // FFN_HALO
        specs += [pl.BlockSpec((FFN_HALO, d), lambda i, f: (jnp.maximum(i * hb - 1, 0), 0))]
        tail_rows = 8
        tail_spec = pl.BlockSpec((8, tf), lambda i, f: (i, f))
        tail_shape = (8 * (m // tm), dff)
    ins += [wa, wg, wd, cw, cb.reshape(1, dff), g.reshape(1, d), b.reshape(1, d)]
    specs += [wcol, wcol, pl.BlockSpec((tf, d), lambda i, f: (f, 0)),
              pl.BlockSpec((3, tf), col), pl.BlockSpec((1, tf), col),
              pl.BlockSpec((1, d), lambda i, f: (0, 0)), pl.BlockSpec((1, d), lambda i, f: (0, 0))]
    return pl.pallas_call(
        functools.partial(_ffn_kernel, tm=tm, sample=sample, t_len=t_len),
        grid=(m // tm, dff // tf),
        in_specs=specs,
        out_specs=[pl.BlockSpec((tm, d), row), tail_spec],
        out_shape=[jax.ShapeDtypeStruct((m, d), F32), jax.ShapeDtypeStruct(tail_shape, F32)],
        scratch_shapes=[pltpu.VMEM((tm + FFN_HALO, d), BF16), pltpu.VMEM((tm, d), F32)],
        compiler_params=_params(("arbitrary", "arbitrary")),
        name="conv_ffn_sample" if sample else "conv_ffn_prompt",
    )(*ins)


def _band_prompt_kernel(*refs, dil, sub, kv_off, head_step, has_sink, has_prev, want_lse):
    it = iter(refs)
    q_ref, kc_ref, kp_ref, vc_ref, vp_ref = (next(it) for _ in range(5))
    sink_ref = next(it) if has_sink else None
    op_ref, lp_ref = (next(it), next(it)) if has_prev else (None, None)
    o_ref = next(it)
    l_ref = next(it) if want_lse else None
    c = pl.program_id(0)
    hg = pl.program_id(1)
    nh = len(kv_off)
    group_scale = jnp.float32(1.0)
    for n in range(1, 4):
        group_scale = jnp.where(hg == n, jnp.float32(head_step ** n), group_scale)
    qi = lax.broadcasted_iota(jnp.int32, (BLK, 2 * BLK), 0)
    kc = lax.broadcasted_iota(jnp.int32, (BLK, 2 * BLK), 1)
    steps = BLK + qi - kc
    in_band = jnp.logical_and(steps >= 0, steps <= BLK)
    in_band_first = jnp.logical_and(in_band, jnp.logical_or(kc >= BLK, c > 0))
    dist = steps.astype(F32) * float(dil)

    def rows(start):
        return pl.ds(start, BLK, stride=dil) if dil > 1 else pl.ds(start, BLK)

    for res in range(dil):
        for w in range(sub):
            cur = rows(res + BLK * w * dil)
            q_all = q_ref[cur, :]
            if w == 0:
                k_prev, v_prev = kp_ref[rows(res), :], vp_ref[rows(res), :]
                valid = in_band_first
            else:
                prv = rows(res + BLK * (w - 1) * dil)
                k_prev, v_prev = kc_ref[prv, :], vc_ref[prv, :]
                valid = in_band
            k_all = jnp.concatenate([k_prev, kc_ref[cur, :]], axis=0).astype(BF16)
            v_all = jnp.concatenate([v_prev, vc_ref[cur, :]], axis=0).astype(BF16)
            if has_prev:
                op_all, lp_all = op_ref[cur, :], lp_ref[cur, :]
            o_parts, l_parts = [], []
            for h in range(nh):
                hs = slice(h * HD, (h + 1) * HD)
                ks = slice(kv_off[h], kv_off[h] + HD)
                slope = _slope(h) * group_scale
                q = q_all[:, hs].astype(BF16)
                s = jnp.where(valid, _nt(q, k_all[:, ks]) * SCALE - slope * dist, NEG)
                m = jnp.max(s, axis=-1, keepdims=True)
                p = jnp.exp(s - m)
                l = jnp.sum(p, axis=-1, keepdims=True)
                if has_sink:
                    l = l + jnp.exp(sink_ref[h] - m)
                o = _mm(p.astype(BF16), v_all[:, ks])
                o = o / l
                lse = m + jnp.log(l)
                if has_prev:
                    lse_prev = lp_all[:, hs]
                    lse_new = jnp.logaddexp(lse_prev, lse)
                    o = op_all[:, hs] * jnp.exp(lse_prev - lse_new) + o * jnp.exp(lse - lse_new)
                    lse = lse_new
                o_parts.append(o)
                l_parts.append(jnp.broadcast_to(lse, (BLK, HD)))
            o_ref[cur, :] = jnp.concatenate(o_parts, axis=1)
            if want_lse:
                l_ref[cur, :] = jnp.concatenate(l_parts, axis=1)


def _band_prompt(proj, dil, sub, q_col, k_col, v_col, nh, kv_w, kv_off, head_step=1.0, sinks=None, prev=None,
                 want_lse=False):
    s, w = proj.shape
    chunk = BLK * dil * sub
    tail = BLK * dil
    qw = nh * HD
    ngroups = 512 // qw
    kv_groups = 1 if kv_w == 128 and nh == 8 else ngroups
    kcol = (lambda g: k_col // kv_w + g) if kv_groups > 1 else (lambda g: k_col // kv_w)
    vcol = (lambda g: v_col // kv_w + g) if kv_groups > 1 else (lambda g: v_col // kv_w)
    qspec = pl.BlockSpec((chunk, qw), lambda c, g: (c, q_col // qw + g))
    kc = pl.BlockSpec((chunk, kv_w), lambda c, g: (c, kcol(g)))
    kp = pl.BlockSpec((tail, kv_w), lambda c, g: (jnp.maximum(c * sub - 1, 0), kcol(g)))
    vc = pl.BlockSpec((chunk, kv_w), lambda c, g: (c, vcol(g)))
    vp = pl.BlockSpec((tail, kv_w), lambda c, g: (jnp.maximum(c * sub - 1, 0), vcol(g)))
    ospec = pl.BlockSpec((chunk, qw), lambda c, g: (c, g))
    ins = [proj, proj, proj, proj, proj]
    specs = [qspec, kc, kp, vc, vp]
    if sinks is not None:
        ins.append(sinks)
        specs.append(pl.BlockSpec(memory_space=pltpu.SMEM))
    if prev is not None:
        ins += [prev[0], prev[1]]
        specs += [ospec, ospec]
    oshape = jax.ShapeDtypeStruct((s, 512), F32)
    return pl.pallas_call(
        functools.partial(_band_prompt_kernel, dil=dil, sub=sub, kv_off=tuple(kv_off), head_step=head_step,
                          has_sink=sinks is not None, has_prev=prev is not None, want_lse=want_lse),
        grid=(s // chunk, ngroups),
        in_specs=specs,
        out_specs=[ospec, ospec] if want_lse else ospec,
        out_shape=[oshape, oshape] if want_lse else oshape,
        compiler_params=_params(("parallel", "parallel")),
        name=f"band_prompt_d{dil}",
    )(*ins)


def _col_consts(values, rows):
    r = lax.broadcasted_iota(jnp.int32, (rows, 1), 0)
    col = jnp.zeros((rows, 1), F32)
    for h, v in enumerate(values):
        col = jnp.where(r == h, v, col)
    return col


def _sample_even_kernel(ps_ref, a1_ref, a2_ref, a3_ref, cb_ref, fold_ref, sink_ref, oa_ref, ob_ref, *, t_len):
    nh = 8
    nr = t_len * nh
    row = lax.broadcasted_iota(jnp.int32, (nr, 1), 0)
    t_col = row // nh
    h_col = row % nh
    hm = (lax.broadcasted_iota(jnp.int32, (nr, nh * HD), 1) // HD == h_col).astype(F32)
    slope = jnp.zeros((nr, 1), F32)
    sink_col = jnp.zeros((nr, 1), F32)
    for h in range(nh):
        slope = jnp.where(h_col == h, _slope(h), slope)
        sink_col = jnp.where(h_col == h, sink_ref[h], sink_col)

    def q_rows(col):
        return jnp.concatenate([jnp.broadcast_to(ps_ref[t:t + 1, col:col + 512], (nh, 512))
                                for t in range(t_len)], axis=0) * hm

    def attend(qk, q_new, kt_ref_slice, vt_ref_slice, k_col, v_col, width, dil, sinks):
        kt = kt_ref_slice.astype(BF16)
        length = kt.shape[1]
        dd = length + t_col - lax.broadcasted_iota(jnp.int32, (nr, length), 1)
        valid = jnp.logical_and((dd & (dil - 1)) == 0, dd <= dil * BLK)
        s = jnp.where(valid, _mm(qk, kt) * SCALE - slope * dd.astype(F32), NEG)
        m = jnp.max(s, axis=-1, keepdims=True)
        new = []
        for tp in range(t_len):
            dn = t_col - tp
            ok = jnp.logical_and(dn >= 0, (dn & (dil - 1)) == 0)
            sc = jnp.sum(q_new * ps_ref[tp:tp + 1, k_col:k_col + width], axis=-1, keepdims=True) * SCALE
            sc = jnp.where(ok, sc - slope * dn.astype(F32), NEG)
            new.append(sc)
            m = jnp.maximum(m, sc)
        p = jnp.exp(s - m)
        l = jnp.sum(p, axis=-1, keepdims=True)
        o = _nt(p.astype(BF16), vt_ref_slice.astype(BF16))
        for tp in range(t_len):
            pn = jnp.exp(new[tp] - m)
            l = l + pn
            o = o + pn * ps_ref[tp:tp + 1, v_col:v_col + width]
        if sinks:
            l = l + jnp.exp(sink_col - m)
        return o / l, m + jnp.log(l)

    def head_rows(o_wide):
        return jnp.sum((o_wide * hm).reshape(t_len, nh, nh * HD), axis=1)

    outs, lses = [], []
    for g, (buf, (_, dil)) in enumerate(zip((a1_ref, a2_ref, a3_ref), A_PATTERNS)):
        q = q_rows(g * 512)
        o, lse = attend(q.astype(BF16), q, buf[0:512, :], buf[512:1024, :],
                        1536 + g * 512, 3072 + g * 512, 512, dil, False)
        outs.append(o)
        lses.append(lse)
    mx = jnp.maximum(jnp.maximum(lses[0], lses[1]), lses[2])
    es = [jnp.exp(l_ - mx) for l_ in lses]
    merged = (es[0] * outs[0] + es[1] * outs[1] + es[2] * outs[2]) / (es[0] + es[1] + es[2])
    oa_ref[...] = head_rows(merged)

    qf = _mm(q_rows(4608).astype(BF16), fold_ref[...])
    o, _ = attend(qf.astype(BF16), qf, cb_ref[0:128, :], cb_ref[128:256, :], 5120, 5248, 128, 1, True)
    o_sw = pltpu.roll(o, HD, 1)
    o_sel = jnp.where((h_col // 4) == (h_col % 2), o, o_sw)
    ob_ref[...] = head_rows(jnp.concatenate([o_sel] * 4, axis=1))


def _fold_matrix():
    c = np.arange(512)[:, None]
    l = np.arange(128)[None, :]
    return jnp.asarray(((c % HD == l % HD) & (l // HD == (c // HD) // 4)).astype(np.float32), BF16)


def _sample_even(ps, a1, a2, a3, cb, layer, sinks, t_len=4):
    nl, n = a1.shape[0], a1.shape[1]
    ps3 = ps.reshape(n, t_len, ps.shape[1])

    def rows_last(c):
        return jnp.transpose(c, (0, 1, 3, 4, 5, 2)).reshape(nl * n, -1, c.shape[2])

    a1v, a2v, a3v, cbv = rows_last(a1), rows_last(a2), rows_last(a3), rows_last(cb)
    b3 = lambda arr: pl.BlockSpec((None,) + arr.shape[1:], lambda i: (layer * n + i, 0, 0))
    ospec = pl.BlockSpec((None, t_len, 512), lambda i: (i, 0, 0))
    oa, ob = pl.pallas_call(
        functools.partial(_sample_even_kernel, t_len=t_len),
        grid=(n,),
        in_specs=[pl.BlockSpec((None, t_len, ps.shape[1]), lambda i: (i, 0, 0)),
                  b3(a1v), b3(a2v), b3(a3v), b3(cbv),
                  pl.BlockSpec((512, 128), lambda i: (0, 0)),
                  pl.BlockSpec(memory_space=pltpu.SMEM)],
        out_specs=[ospec, ospec],
        out_shape=[jax.ShapeDtypeStruct((n, t_len, 512), F32)] * 2,
        compiler_params=_params(("parallel",)),
        name="sample_even_attn",
    )(ps3, a1v, a2v, a3v, cbv, _fold_matrix(), sinks)
    return oa.reshape(n * t_len, 512), ob.reshape(n * t_len, 512)


C_GROUP = C_HEADS // C_KV
D_GROUP = D_HEADS // D_KV


def _lambda(lamv, lambda_init):
    s1 = jnp.sum(lamv[0:1, :] * lamv[1:2, :], axis=-1, keepdims=True)
    s2 = jnp.sum(lamv[2:3, :] * lamv[3:4, :], axis=-1, keepdims=True)
    return jnp.exp(s1) - jnp.exp(s2) + lambda_init


def _sub_rms(o0, o1, lam, g, lambda_init):
    d = o0 - lam * o1
    return d * lax.rsqrt(jnp.mean(d * d, axis=-1, keepdims=True) + LN_EPS) * g * (1.0 - lambda_init)


def _softplus(z):
    return jnp.maximum(z, 0.0) + jnp.log(1.0 + jnp.exp2(jnp.abs(z) * (-math.log2(math.e))))


def _tri_matrix(n):
    u = (np.arange(n)[:, None] >= np.arange(n)[None, :]).astype(np.float32)
    return jnp.asarray(np.concatenate([u, u], axis=0), BF16)


def _rev_cumsum(sp, uu):
    hi = sp.astype(BF16)
    lo = (sp - hi.astype(F32)).astype(BF16)
    return _mm(jnp.concatenate([hi, lo], axis=1), uu)


def _matmul_t_kernel(wt_ref, x_ref, o_ref):
    o_ref[...] = _nt(wt_ref[...], x_ref[...].astype(BF16))


def _matmul_t(wt, x, tm=512):
    n, k = wt.shape
    m = x.shape[0]
    tm = min(tm, m)
    return pl.pallas_call(
        _matmul_t_kernel,
        grid=(m // tm,),
        in_specs=[pl.BlockSpec((n, k), lambda i: (0, 0)), pl.BlockSpec((tm, k), lambda i: (i, 0))],
        out_specs=pl.BlockSpec((n, tm), lambda i: (0, i)),
        out_shape=jax.ShapeDtypeStruct((n, m), F32),
        compiler_params=_params(("parallel",)),
        name="proj_matmul_t",
    )(wt, x)


def _diff_prompt_kernel(q_ref, k_ref, vt_ref, lamv_ref, g_ref, o_ref, b0_ref, m_ref, l_ref, acc_ref, *, tq, lambda_init):
    hk = pl.program_id(0)
    i = pl.program_id(1)
    rows = C_GROUP * tq
    lane = lax.broadcasted_iota(jnp.int32, (1, rows), 1)
    slope = jnp.zeros((1, rows), F32)
    for g in range(C_GROUP):
        slope = jnp.where(lane // tq == g, _slope(g), slope)
    slope = slope * jnp.where(hk == 0, 1.0, 2.0 ** (-C_GROUP)).astype(F32)
    kl = lax.broadcasted_iota(jnp.int32, (tq, rows), 0)
    ql = lax.broadcasted_iota(jnp.int32, (tq, rows), 1) % tq
    b0_ref[...] = slope * (kl - ql).astype(F32)
    qs = []
    for m in range(2):
        parts = [q_ref[:, (g * 2 + m) * HD:(g * 2 + m + 1) * HD] for g in range(C_GROUP)]
        qs.append((jnp.concatenate(parts, axis=0) * SCALE).astype(BF16))
    m_ref[...] = jnp.full(m_ref.shape, NEG, F32)
    l_ref[...] = jnp.zeros_like(l_ref)
    acc_ref[...] = jnp.zeros_like(acc_ref)

    def blocks(js, diagonal):
        ks, vts, cjs = [], [], []
        for j in js:
            start = pl.multiple_of(j * tq, tq)
            ks.append(k_ref[pl.ds(start, tq), :].astype(BF16))
            vts.append(vt_ref[:, pl.ds(start, tq)].astype(BF16))
            cjs.append(-slope * ((i - j) * tq).astype(F32))
        for m in range(2):
            ss = []
            m_old = m_ref[m]
            m_new = m_old
            for k, cj in zip(ks, cjs):
                s = _nt(k[:, m * HD:(m + 1) * HD], qs[m]) + b0_ref[...]
                if diagonal:
                    s = jnp.where(kl <= ql, s, NEG)
                ss.append(s)
                m_new = jnp.maximum(m_new, jnp.max(s, axis=0, keepdims=True) + cj)
            alpha = jnp.exp(m_old - m_new)
            l = alpha * l_ref[m]
            acc = alpha * acc_ref[m]
            for s, vt, cj in zip(ss, vts, cjs):
                p = jnp.exp(s + (cj - m_new))
                l = l + jnp.sum(p, axis=0, keepdims=True)
                acc = acc + _mm(vt, p.astype(BF16))
            l_ref[m] = l
            acc_ref[m] = acc
            m_ref[m] = m_new

    @pl.when(i % 2 == 1)
    def _():
        blocks((0,), False)

    def off_diagonal(it, carry):
        j = i % 2 + 2 * it
        blocks((j, j + 1), False)
        return carry

    lax.fori_loop(0, i // 2, off_diagonal, 0)
    blocks((i,), True)
    lam = _lambda(lamv_ref[...], lambda_init)
    d = acc_ref[0] / l_ref[0] - lam * (acc_ref[1] / l_ref[1])
    d = d * lax.rsqrt(jnp.mean(d * d, axis=0, keepdims=True) + LN_EPS) * g_ref[...] * (1.0 - lambda_init)
    for g in range(C_GROUP):
        o_ref[:, g * 2 * HD:(g + 1) * 2 * HD] = d[:, g * tq:(g + 1) * tq].T


def _diff_prompt(proj, vt, lamv, subln_g, lambda_init, tq=256):
    s = proj.shape[0]
    rows = C_GROUP * tq
    return pl.pallas_call(
        functools.partial(_diff_prompt_kernel, tq=tq, lambda_init=lambda_init),
        grid=(C_KV, s // tq),
        in_specs=[pl.BlockSpec((tq, 512), lambda h, i: (i, h)),
                  pl.BlockSpec((s, 128), lambda h, i: (0, 8 + h)),
                  pl.BlockSpec((2 * HD, s), lambda h, i: (h, 0)),
                  pl.BlockSpec((4, HD), lambda h, i: (0, 0)),
                  pl.BlockSpec((2 * HD, 1), lambda h, i: (0, 0))],
        out_specs=pl.BlockSpec((tq, 512), lambda h, i: (i, h)),
        out_shape=jax.ShapeDtypeStruct((s, C_HEADS * 2 * HD), F32),
        scratch_shapes=[pltpu.VMEM((tq, rows), F32), pltpu.VMEM((2, 1, rows), F32), pltpu.VMEM((2, 1, rows), F32),
                        pltpu.VMEM((2, 2 * HD, rows), F32)],
        compiler_params=_params(("parallel", "arbitrary")),
        name="diff_attn_prompt",
    )(proj, proj, vt, lamv, subln_g.reshape(2 * HD, 1))


def _tri_matrix_t(n):
    u = (np.arange(n)[None, :] >= np.arange(n)[:, None]).astype(np.float32)
    return jnp.asarray(np.concatenate([u, u], axis=1), BF16)


def _sb_prompt_kernel(q_ref, k_ref, vt_ref, ut_ref, o_ref, c_ref, acc_ref, *, tq):
    i = pl.program_id(1)
    rows = D_GROUP * tq
    kl = lax.broadcasted_iota(jnp.int32, (tq, rows), 0)
    ql = lax.broadcasted_iota(jnp.int32, (tq, rows), 1) % tq
    qs = []
    for hl in range(2):
        parts = [q_ref[:, (hl * D_GROUP + g) * HD:(hl * D_GROUP + g + 1) * HD] for g in range(D_GROUP)]
        qs.append((jnp.concatenate(parts, axis=0) * SCALE).astype(BF16))
    c_ref[...] = jnp.zeros_like(c_ref)
    acc_ref[...] = jnp.zeros_like(acc_ref)

    def blocks(js, diagonal):
        for hl in range(2):
            c = c_ref[hl]
            pv = None
            for j in js:
                start = pl.multiple_of(j * tq, tq)
                k = k_ref[pl.ds(start, tq), hl * HD:(hl + 1) * HD].astype(BF16)
                vt = vt_ref[hl * HD:(hl + 1) * HD, pl.ds(start, tq)].astype(BF16)
                z = _nt(k, qs[hl])
                sp = _softplus(z)
                if diagonal:
                    sp = jnp.where(kl < ql, sp, 0.0)
                hi = sp.astype(BF16)
                lo = (sp - hi.astype(F32)).astype(BF16)
                tl = _mm(ut_ref[...], jnp.concatenate([hi, lo], axis=0))
                e = z - tl - c
                if diagonal:
                    e = jnp.where(kl < ql, e, NEG)
                term = _mm(vt, jnp.exp(e).astype(BF16))
                pv = term if pv is None else pv + term
                c = c + tl[0:1, :]
            acc_ref[hl * HD:(hl + 1) * HD, :] += pv
            c_ref[hl] = c

    blocks((i,), True)

    @pl.when(i % 2 == 1)
    def _():
        blocks((i - 1,), False)

    top = i - 1 - i % 2

    def older(it, carry):
        blocks((top - 2 * it, top - 2 * it - 1), False)
        return carry

    lax.fori_loop(0, i // 2, older, 0)
    for g in range(D_GROUP):
        t = acc_ref[:, g * tq:(g + 1) * tq].T
        for hl in range(2):
            o_ref[:, (hl * D_GROUP + g) * HD:(hl * D_GROUP + g + 1) * HD] = t[:, hl * HD:(hl + 1) * HD]


def _sb_prompt(proj, vt, tq=256):
    s = proj.shape[0]
    rows = D_GROUP * tq
    return pl.pallas_call(
        functools.partial(_sb_prompt_kernel, tq=tq),
        grid=(D_KV // 2, s // tq),
        in_specs=[pl.BlockSpec((tq, 512), lambda h, i: (i, 3 + h)),
                  pl.BlockSpec((s, 128), lambda h, i: (0, 20 + h)),
                  pl.BlockSpec((2 * HD, s), lambda h, i: (2 + h, 0)),
                  pl.BlockSpec((tq, 2 * tq), lambda h, i: (0, 0))],
        out_specs=pl.BlockSpec((tq, 512), lambda h, i: (i, h)),
        out_shape=jax.ShapeDtypeStruct((s, D_HEADS * HD), F32),
        scratch_shapes=[pltpu.VMEM((2, 1, rows), F32), pltpu.VMEM((2 * HD, rows), F32)],
        compiler_params=_params(("parallel", "arbitrary")),
        name="sb_attn_prompt",
    )(proj, proj, vt, _tri_matrix_t(tq))


PAGES_PER_STEP = 8
QROWS = 64


def _sample_odd_kernel(pt_ref, qc_ref, qd_ref, ps_ref, *refs, npages, t_len, past_len, lambda_init):
    pp = PAGES_PER_STEP
    hrows = QROWS // C_KV
    pools = refs[0:4]
    uu_ref, lamv_ref, g_ref, oc_ref, od_ref = refs[4:9]
    bufs = refs[9:13]
    sem, m_ref, l_ref, accc_ref, c_ref, accd_ref = refs[13:]
    b = pl.program_id(0)
    j = pl.program_id(1)
    nsteps = pl.num_programs(1)
    slot = j % 2

    def page_copies(bb, jj, sl):
        out = []
        for i in range(pp):
            page = pt_ref[bb, npages - 1 - (jj * pp + i)]
            for pool, buf in zip(pools, bufs):
                out.append(pltpu.make_async_copy(pool.at[page], buf.at[sl, i], sem.at[sl]))
        return out

    def start_all(copies):
        for n, cp in enumerate(copies):
            cp.start(priority=n % 2)

    @pl.when(jnp.logical_and(b == 0, j == 0))
    def _():
        start_all(page_copies(0, 0, 0))
        start_all(page_copies(0, 1, 1))

    for cp in page_copies(b, j, slot):
        cp.wait()
    kc_refs = [bufs[0].at[slot, i] for i in range(pp)]
    vc_refs = [bufs[1].at[slot, i] for i in range(pp)]
    kd_refs = [bufs[2].at[slot, i] for i in range(pp)]
    vd_refs = [bufs[3].at[slot, i] for i in range(pp)]
    r = lax.broadcasted_iota(jnp.int32, (QROWS, 1), 0)
    tc_col = (r // C_GROUP) % t_len
    td_col = (r // D_GROUP) % t_len
    head_c = (r // (2 * t_len * C_GROUP)) * C_GROUP + r % C_GROUP
    slope = jnp.zeros((QROWS, 1), F32)
    for h in range(C_HEADS):
        slope = jnp.where(head_c == h, _slope(h), slope)
    qc = qc_ref[...] * SCALE
    qd = qd_ref[...] * SCALE

    @pl.when(j == 0)
    def _():
        def per_head(tp, cols):
            return jnp.concatenate(
                [jnp.broadcast_to(ps_ref[tp:tp + 1, cols + hk * 128:cols + (hk + 1) * 128], (hrows, 128))
                 for hk in range(C_KV)], axis=0)

        scs = []
        for tp in range(t_len):
            sc = jnp.sum(qc * per_head(tp, 1024), axis=-1, keepdims=True)
            sc = sc - slope * (tc_col - tp).astype(F32)
            scs.append(jnp.where(tp <= tc_col, sc, NEG))
        m = scs[0]
        for sc in scs[1:]:
            m = jnp.maximum(m, sc)
        l = jnp.zeros((QROWS, 1), F32)
        acc = jnp.zeros((QROWS, 128), F32)
        for tp in range(t_len):
            p = jnp.exp(scs[tp] - m)
            l = l + p
            acc = acc + p * per_head(tp, 1280)
        m_ref[...] = m
        l_ref[...] = l
        accc_ref[...] = acc
        c = jnp.zeros((QROWS, 1), F32)
        acc = jnp.zeros((QROWS, 256), F32)
        for tp in range(t_len - 1, -1, -1):
            z = jnp.sum(qd * ps_ref[tp:tp + 1, 2560:2816], axis=-1, keepdims=True)
            earlier = tp < td_col
            sp = jnp.where(earlier, _softplus(z), 0.0)
            a = jnp.exp(jnp.where(earlier, z - sp - c, NEG))
            acc = acc + a * ps_ref[tp:tp + 1, 2816:3072]
            c = c + sp
        c_ref[...] = c
        accd_ref[...] = acc

    qcb = qc.astype(BF16)
    qpos = (past_len + tc_col).astype(F32)
    col = lax.broadcasted_iota(jnp.int32, (1, PAGE * C_KV), 1)
    own_head = (col % C_KV) == (r // hrows)
    ckey = col // C_KV
    ss = []
    for i in range(pp):
        page = npages - 1 - (j * pp + i)
        kpos = (page * PAGE + ckey).astype(F32)
        sc = _nt(qcb, kc_refs[i][...].astype(BF16))
        ss.append(jnp.where(own_head, sc - slope * (qpos - kpos), NEG))
    s = jnp.concatenate(ss, axis=1)
    m_old = m_ref[...]
    m_new = jnp.maximum(m_old, jnp.max(s, axis=-1, keepdims=True))
    alpha = jnp.exp(m_old - m_new)
    p = jnp.exp(s - m_new)
    l_ref[...] = alpha * l_ref[...] + jnp.sum(p, axis=-1, keepdims=True)
    p = p.astype(BF16)
    w = PAGE * C_KV
    pv = _mm(p[:, 0:w], vc_refs[0][...].astype(BF16))
    for i in range(1, pp):
        pv = pv + _mm(p[:, i * w:(i + 1) * w], vc_refs[i][...].astype(BF16))
    accc_ref[...] = alpha * accc_ref[...] + pv
    m_ref[...] = m_new

    qdb = qd.astype(BF16)
    kd_all = jnp.concatenate([kd_refs[i][...].astype(BF16) for i in range(pp)], axis=1)
    z = _mm(qdb, kd_all)
    sp = _softplus(z)
    hi = sp.astype(BF16)
    lo = (sp - hi.astype(F32)).astype(BF16)
    stacked = jnp.concatenate(
        [jnp.concatenate([hi[:, i * PAGE:(i + 1) * PAGE], lo[:, i * PAGE:(i + 1) * PAGE]], axis=1)
         for i in range(pp)], axis=0)
    tl_all = _mm(stacked, uu_ref[...])
    c = c_ref[...]
    acc = accd_ref[...]
    for i in range(pp):
        tl = tl_all[i * QROWS:(i + 1) * QROWS]
        a = jnp.exp(z[:, i * PAGE:(i + 1) * PAGE] - tl - c)
        acc = acc + _nt(a.astype(BF16), vd_refs[i][...].astype(BF16))
        c = c + tl[:, 0:1]
    c_ref[...] = c
    accd_ref[...] = acc

    two_ahead = jnp.logical_not(jnp.logical_and(b == pl.num_programs(0) - 1, j >= nsteps - 2))

    @pl.when(two_ahead)
    def _():
        wrap = j + 2 >= nsteps
        start_all(page_copies(jnp.where(wrap, b + 1, b), jnp.where(wrap, j + 2 - nsteps, j + 2), slot))

    @pl.when(j == pl.num_programs(1) - 1)
    def _():
        lam = _lambda(lamv_ref[...], lambda_init)
        o = accc_ref[...] / l_ref[...]
        half = t_len * C_GROUP
        for hk in range(C_KV):
            o0 = o[(hk * 2) * half:(hk * 2 + 1) * half, :]
            o1 = o[(hk * 2 + 1) * half:(hk * 2 + 2) * half, :]
            oc_ref[hk * half:(hk + 1) * half, :] = _sub_rms(o0, o1, lam, g_ref[...], lambda_init)
        per = t_len * D_GROUP
        accd = accd_ref[...]
        for hk in range(D_KV):
            od_ref[hk * per:(hk + 1) * per, :] = accd[hk * per:(hk + 1) * per, hk * HD:(hk + 1) * HD]


def _sample_odd(ps, pools, page_base, page_table, lamv, subln_g, lambda_init, t_len=4):
    n, npages = page_table.shape
    pp = PAGES_PER_STEP
    ps3 = ps.reshape(n, t_len, ps.shape[1])
    eye2 = jnp.eye(2, dtype=F32)
    eye4 = jnp.eye(4, dtype=F32)
    qc = ps3[:, :, 0:1024].reshape(n, t_len, C_KV, C_GROUP, 2, HD).transpose(0, 2, 4, 1, 3, 5)
    qc = (qc[:, :, :, :, :, None, :] * eye2[None, None, :, None, None, :, None]).reshape(n, QROWS, 128)
    qd = ps3[:, :, 1536:2560].reshape(n, t_len, D_KV, D_GROUP, HD).transpose(0, 2, 1, 3, 4)
    qd = (qd[:, :, :, :, None, :] * eye4[None, :, None, None, :, None]).reshape(n, QROWS, 256)
    pt = page_table + page_base

    nsteps = npages // pp
    assert nsteps % 2 == 0, "the page double buffer keys its slot on the step parity"
    per_b = lambda w, rws: pl.BlockSpec((None, rws, w), lambda b, j, pt_ref: (b, 0, 0))
    fixed = lambda shape: pl.BlockSpec(shape, lambda b, j, pt_ref: (0, 0))
    in_specs = [per_b(128, QROWS), per_b(256, QROWS), per_b(ps.shape[1], t_len)]
    in_specs += [pl.BlockSpec(memory_space=pl.ANY)] * 4
    in_specs += [fixed((2 * PAGE, PAGE)), fixed((4, HD)), fixed((1, 2 * HD))]
    ins = [qc, qd, ps3, *pools, _tri_matrix(PAGE), lamv, subln_g.reshape(1, 2 * HD)]
    page_buf = pltpu.VMEM((2, pp, 256, PAGE), F32)
    oc, od = pl.pallas_call(
        functools.partial(_sample_odd_kernel, npages=npages, t_len=t_len, past_len=npages * PAGE,
                          lambda_init=lambda_init),
        grid_spec=pltpu.PrefetchScalarGridSpec(
            num_scalar_prefetch=1,
            grid=(n, nsteps),
            in_specs=in_specs,
            out_specs=[per_b(2 * HD, C_KV * t_len * C_GROUP), per_b(HD, D_KV * t_len * D_GROUP)],
            scratch_shapes=[page_buf, page_buf, page_buf, page_buf, pltpu.SemaphoreType.DMA((2,)),
                            pltpu.VMEM((QROWS, 1), F32), pltpu.VMEM((QROWS, 1), F32), pltpu.VMEM((QROWS, 128), F32),
                            pltpu.VMEM((QROWS, 1), F32), pltpu.VMEM((QROWS, 256), F32)]),
        out_shape=[jax.ShapeDtypeStruct((n, C_KV * t_len * C_GROUP, 2 * HD), F32),
                   jax.ShapeDtypeStruct((n, D_KV * t_len * D_GROUP, HD), F32)],
        compiler_params=_params(("arbitrary", "arbitrary")),
        name="sample_odd_attn",
    )(pt, *ins)
    oc = oc.reshape(n, C_KV, t_len, C_GROUP, 2 * HD).transpose(0, 2, 1, 3, 4).reshape(n * t_len, C_HEADS * 2 * HD)
    od = od.reshape(n, D_KV, t_len, D_GROUP, HD).transpose(0, 2, 1, 3, 4).reshape(n * t_len, D_HEADS * HD)
    return oc, od


def kernel(x_prompt, x_sample, cache_a1, cache_a2, cache_a3, cache_b, cache_c_k, cache_c_v, cache_d_k, cache_d_v,
           state_conv, page_table, w_in_even, w_out_even, sinks_b, w_in_odd, w_out_odd,
           lam_q1, lam_k1, lam_q2, lam_k2, subln_g, w_ffn_a, conv_w, conv_b, w_ffn_g, w_ffn_down,
           ln_mix_g, ln_mix_b, ln_ffn_g, ln_ffn_b):
    bsz, seq, d = x_prompt.shape
    nb, t_len, _ = x_sample.shape
    assert bsz == 1, "the prompt group is one sequence"
    xp = x_prompt.reshape(seq, d)
    xs = x_sample.reshape(nb * t_len, d)
    b_off = [(h // (B_HEADS // B_KV)) * HD for h in range(B_HEADS)]
    n_phys = cache_c_k.shape[1]
    pools = [c.reshape(c.shape[0] * n_phys, PAGE * C_KV, 2 * HD) for c in (cache_c_k, cache_c_v)]
    pools += [jnp.transpose(c, (0, 1, 3, 4, 2)).reshape(c.shape[0] * n_phys, D_KV * HD, PAGE)
              for c in (cache_d_k, cache_d_v)]
    even_p, even_s, odd_p, odd_s, conv_p, conv_s = [], [], [], [], [], []
    for layer in range(DEPTH):
        i = layer // 2
        if layer % 2 == 0:
            w_in = jnp.pad(w_in_even[i].astype(BF16), ((0, 0), (0, EVEN_PAD - EVEN_IN)))
            w_out = w_out_even[i].astype(BF16)
            pp = _matmul(xp, w_in, 512, 1408)
            ps = _matmul(xs, w_in, 512, 1408)
            prev = None
            for g, (_, dil) in enumerate(A_PATTERNS):
                last = g == len(A_PATTERNS) - 1
                res = _band_prompt(pp, dil, 4 if dil == 1 else 1, g * 512, 1536 + g * 512, 3072 + g * 512,
                                   2, 128, (0, HD), head_step=0.25, prev=prev, want_lse=not last)
                prev = None if last else res
            oa_p = res
            ob_p = _band_prompt(pp, 1, 4, 4608, 5120, 5248, B_HEADS, 128, b_off, sinks=sinks_b[i].reshape(B_HEADS))
            oa_s, ob_s = _sample_even(ps, cache_a1, cache_a2, cache_a3, cache_b, i, sinks_b[i].reshape(B_HEADS), t_len)
            w1, w2 = w_out[:512], w_out[512:]
            rows_p, rows_s = [], []
            for g, (win, _) in enumerate(A_PATTERNS):
                w = min(win, seq)
                kv = jnp.stack([pp[seq - w:, 1536 + g * 512:2048 + g * 512].reshape(w, A_HEADS, HD),
                                pp[seq - w:, 3072 + g * 512:3584 + g * 512].reshape(w, A_HEADS, HD)], axis=1)
                rows_p.append(kv[None])
                rows_s.append(jnp.stack([ps[:, 1536 + g * 512:2048 + g * 512].reshape(nb, t_len, A_HEADS, HD),
                                         ps[:, 3072 + g * 512:3584 + g * 512].reshape(nb, t_len, A_HEADS, HD)], axis=2))
            w = min(128, seq)
            rows_p.append(jnp.stack([pp[seq - w:, 5120:5248].reshape(w, B_KV, HD),
                                     pp[seq - w:, 5248:5376].reshape(w, B_KV, HD)], axis=1)[None])
            rows_s.append(jnp.stack([ps[:, 5120:5248].reshape(nb, t_len, B_KV, HD),
                                     ps[:, 5248:5376].reshape(nb, t_len, B_KV, HD)], axis=2))
            even_p.append(rows_p)
            even_s.append(rows_s)
            mix_p, mix_s = (oa_p, ob_p), (oa_s, ob_s)
        else:
            lambda_init = 0.8 - 0.6 * math.exp(-0.3 * layer)
            lamv = jnp.stack([lam_q1[i], lam_k1[i], lam_q2[i], lam_k2[i]])
            w_in = w_in_odd[i].astype(BF16)
            w_out = w_out_odd[i].astype(BF16)
            pp = _matmul(xp, w_in, 512, 1024)
            ps = _matmul(xs, w_in, 512, 1024)
            wvt = jnp.concatenate([w_in[:, 1280:1536], w_in[:, 2816:3072]], axis=1).T
            vt = _matmul_t(wvt, xp)
            oc_p = _diff_prompt(pp, vt, lamv, subln_g[i], lambda_init)
            od_p = _sb_prompt(pp, vt)
            oc_s, od_s = _sample_odd(ps, pools, i * n_phys, page_table, lamv, subln_g[i], lambda_init, t_len)
            w1, w2 = w_out[:1024], w_out[1024:]
            odd_p.append((pp[:, 1024:1280].reshape(1, seq, C_KV, 2 * HD), pp[:, 1280:1536].reshape(1, seq, C_KV, 2 * HD),
                          pp[:, 2560:2816].reshape(1, seq, D_KV, HD), pp[:, 2816:3072].reshape(1, seq, D_KV, HD)))
            odd_s.append((ps[:, 1024:1280].reshape(nb, t_len, C_KV, 2 * HD), ps[:, 1280:1536].reshape(nb, t_len, C_KV, 2 * HD),
                          ps[:, 2560:2816].reshape(nb, t_len, D_KV, HD), ps[:, 2816:3072].reshape(nb, t_len, D_KV, HD)))
            mix_p, mix_s = (oc_p, od_p), (oc_s, od_s)
        xp = _outproj_ln(mix_p[0], mix_p[1], w1, w2, xp, ln_mix_g[layer], ln_mix_b[layer])
        xs = _outproj_ln(mix_s[0], mix_s[1], w1, w2, xs, ln_mix_g[layer], ln_mix_b[layer])
        wa, wg, wd = w_ffn_a[layer].astype(BF16), w_ffn_g[layer].astype(BF16), w_ffn_down[layer].astype(BF16)
        ffn_args = (wa, wg, wd, conv_w[layer], conv_b[layer], ln_ffn_g[layer], ln_ffn_b[layer])
        xp, tail_p = _ffn(xp, *ffn_args)
        st = state_conv[layer]
        p1 = jnp.concatenate([st[:, 1:2], jnp.zeros((nb, t_len - 1, D_FF), F32)], axis=1).reshape(nb * t_len, D_FF)
        p2 = jnp.concatenate([st, jnp.zeros((nb, t_len - 2, D_FF), F32)], axis=1).reshape(nb * t_len, D_FF)
        xs, tail_s = _ffn(xs, *ffn_args, prev=(p1, p2), t_len=t_len)
        conv_p.append(tail_p[-2:][None])
        conv_s.append(tail_s.reshape(nb, t_len, D_FF)[:, t_len - 2:])
    stack = lambda per_layer, j: jnp.stack([entry[j] for entry in per_layer])
    outs = [xp.reshape(1, seq, d), xs.reshape(nb, t_len, d)]
    for j in range(4):
        outs += [stack(even_p, j), stack(even_s, j)]
    for j in range(4):
        outs += [stack(odd_p, j), stack(odd_s, j)]
    outs += [jnp.stack(conv_p), jnp.stack(conv_s)]
    return tuple(outs)
```

```python
import functools
import math

import numpy as np
import jax
import jax.numpy as jnp
from jax import lax
from jax.experimental import pallas as pl
from jax.experimental.pallas import tpu as pltpu

F32 = jnp.float32
BF16 = jnp.bfloat16

D_MODEL = 2048
DEPTH = 2
PAGE = 128
HD = 64
BLK = 128
A_PATTERNS = ((128, 1), (512, 4), (2048, 16))
A_HEADS = 8
B_HEADS = 8
B_KV = 2
C_HEADS = 8
C_KV = 2
D_HEADS = 16
D_KV = 4
D_FF = 5632
LN_EPS = 1e-5
NEG = -1e30
ALPHA = (2 * DEPTH) ** 0.25
EVEN_IN = 5376
EVEN_PAD = 5632
ODD_IN = 3072
SCALE = HD ** -0.5

VMEM_LIMIT = 56 * 1024 * 1024


def _params(sem):
    return pltpu.CompilerParams(dimension_semantics=sem, vmem_limit_bytes=VMEM_LIMIT)


def _nt(a, b):
    return lax.dot_general(a, b, (((1,), (1,)), ((), ())), preferred_element_type=F32)


def _mm(a, b):
    return jnp.dot(a, b, preferred_element_type=F32)


def _slope(h, n=8):
    return 2.0 ** (-8.0 * (h + 1) / n)


def _matmul_kernel(x_ref, w_ref, o_ref, xb_ref):
    @pl.when(pl.program_id(1) == 0)
    def _():
        xb_ref[...] = x_ref[...].astype(BF16)

    o_ref[...] = _mm(xb_ref[...], w_ref[...])


def _matmul(x, w, tm, tn):
    m, k = x.shape
    n = w.shape[1]
    tm = min(tm, m)
    return pl.pallas_call(
        _matmul_kernel,
        grid=(m // tm, n // tn),
        in_specs=[pl.BlockSpec((tm, k), lambda i, j: (i, 0)),
                  pl.BlockSpec((k, tn), lambda i, j: (0, j))],
        out_specs=pl.BlockSpec((tm, tn), lambda i, j: (i, j)),
        out_shape=jax.ShapeDtypeStruct((m, n), F32),
        scratch_shapes=[pltpu.VMEM((tm, k), BF16)],
        compiler_params=_params(("parallel", "arbitrary")),
        name="proj_matmul",
    )(x, w)


def _layer_norm_rows(r, g, b):
    mu = jnp.mean(r, axis=-1, keepdims=True)
    d = r - mu
    var = jnp.mean(d * d, axis=-1, keepdims=True)
    return d * lax.rsqrt(var + LN_EPS) * g + b


def _outproj_ln_kernel(m1_ref, m2_ref, w1_ref, w2_ref, x_ref, g_ref, b_ref, o_ref):
    y = _mm(m1_ref[...].astype(BF16), w1_ref[...]) + _mm(m2_ref[...].astype(BF16), w2_ref[...])
    o_ref[...] = _layer_norm_rows(ALPHA * x_ref[...] + y, g_ref[...], b_ref[...])


def _outproj_ln(m1, m2, w1, w2, x, g, b, tm=256):
    m, d = x.shape
    k1, k2 = m1.shape[1], m2.shape[1]
    tm = min(tm, m)
    row = lambda i: (i, 0)
    fixed = lambda i: (0, 0)
    return pl.pallas_call(
        _outproj_ln_kernel,
        grid=(m // tm,),
        in_specs=[pl.BlockSpec((tm, k1), row), pl.BlockSpec((tm, k2), row),
                  pl.BlockSpec((k1, d), fixed), pl.BlockSpec((k2, d), fixed),
                  pl.BlockSpec((tm, d), row), pl.BlockSpec((1, d), fixed), pl.BlockSpec((1, d), fixed)],
        out_specs=pl.BlockSpec((tm, d), row),
        out_shape=jax.ShapeDtypeStruct((m, d), F32),
        compiler_params=_params(("parallel",)),
        name="outproj_ln",
    )(m1, m2, w1, w2, x, g.reshape(1, d), b.reshape(1, d))


FFN_HALO = 16


def _ffn_kernel(*refs, tm, sample, t_len):
    if sample:
        x_ref, p1_ref, p2_ref, wa_ref, wg_ref, wd_ref, cw_ref, cb_ref, g_ref, b_ref, o_ref, tail_ref, xb_ref, acc_ref = refs
    else:
        x_ref, xh_ref, wa_ref, wg_ref, wd_ref, cw_ref, cb_ref, g_ref, b_ref, o_ref, tail_ref, xb_ref, acc_ref = refs
    i = pl.program_id(0)
    f = pl.program_id(1)

    @pl.when(f == 0)
    def _():
        xb_ref[FFN_HALO:, :] = x_ref[...].astype(BF16)
        if sample:
            xb_ref[:FFN_HALO, :] = jnp.zeros((FFN_HALO, x_ref.shape[1]), BF16)
        else:
            halo = jnp.where(i > 0, xh_ref[...], 0.0)
            xb_ref[:FFN_HALO, :] = halo.astype(BF16)
        acc_ref[...] = jnp.zeros_like(acc_ref)

    a_ext = _mm(xb_ref[...], wa_ref[...])
    gate = _mm(xb_ref[FFN_HALO:, :], wg_ref[...])
    a = a_ext[FFN_HALO:, :]
    a1 = pltpu.roll(a_ext, 1, 0)[FFN_HALO:, :]
    a2 = pltpu.roll(a_ext, 2, 0)[FFN_HALO:, :]
    if sample:
        t = lax.broadcasted_iota(jnp.int32, a.shape, 0) % t_len
        a1 = jnp.where(t == 0, p1_ref[...], a1)
        a2 = jnp.where(t < 2, p2_ref[...], a2)
    cw = cw_ref[...]
    c = a2 * cw[0:1, :] + a1 * cw[1:2, :] + a * cw[2:3, :] + cb_ref[...]
    h = (c * jax.nn.sigmoid(c)) * gate
    acc_ref[...] += _mm(h.astype(BF16), wd_ref[...])
    tail_ref[...] = a[tm - tail_ref.shape[0]:, :]

    @pl.when(f == pl.num_programs(1) - 1)
    def _():
        o_ref[...] = _layer_norm_rows(ALPHA * x_ref[...] + acc_ref[...], g_ref[...], b_ref[...])


def _ffn(x, wa, wg, wd, cw, cb, g, b, prev=None, t_len=4, tm=512, tf=512):
    m, d = x.shape
    dff = wa.shape[1]
    tm = min(tm, m)
    sample = prev is not None
    row = lambda i, f: (i, 0)
    col = lambda i, f: (0, f)
    wcol = pl.BlockSpec((d, tf), col)
    ins = [x]
    specs = [pl.BlockSpec((tm, d), row)]
    if sample:
        ins += list(prev)
        specs += [pl.BlockSpec((tm, tf), lambda i, f: (i, f))] * 2
        tail_rows = tm
        tail_spec = pl.BlockSpec((tm, tf), lambda i, f: (i, f))
        tail_shape = (m, dff)
    else:
        ins += [x]
        hb = tm // FFN_HALO
        specs += [pl.BlockSpec((FFN_HALO, d), lambda i, f: (jnp.maximum(i * hb - 1, 0), 0))]
        tail_rows = 8
        tail_spec = pl.BlockSpec((8, tf), lambda i, f: (i, f))
        tail_shape = (8 * (m // tm), dff)
    ins += [wa, wg, wd, cw, cb.reshape(1, dff), g.reshape(1, d), b.reshape(1, d)]
    specs += [wcol, wcol, pl.BlockSpec((tf, d), lambda i, f: (f, 0)),
              pl.BlockSpec((3, tf), col), pl.BlockSpec((1, tf), col),
              pl.BlockSpec((1, d), lambda i, f: (0, 0)), pl.BlockSpec((1, d), lambda i, f: (0, 0))]
    return pl.pallas_call(
        functools.partial(_ffn_kernel, tm=tm, sample=sample, t_len=t_len),
        grid=(m // tm, dff // tf),
        in_specs=specs,
        out_specs=[pl.BlockSpec((tm, d), row), tail_spec],
        out_shape=[jax.ShapeDtypeStruct((m, d), F32), jax.ShapeDtypeStruct(tail_shape, F32)],
        scratch_shapes=[pltpu.VMEM((tm + FFN_HALO, d), BF16), pltpu.VMEM((tm, d), F32)],
        compiler_params=_params(("arbitrary", "arbitrary")),
        name="conv_ffn_sample" if sample else "conv_ffn_prompt",
    )(*ins)


def _band_prompt_kernel(*refs, dil, sub, kv_off, head_step, has_sink, has_prev, want_lse):
    it = iter(refs)
    q_ref, kc_ref, kp_ref, vc_ref, vp_ref = (next(it) for _ in range(5))
    sink_ref = next(it) if has_sink else None
    op_ref, lp_ref = (next(it), next(it)) if has_prev else (None, None)
    o_ref = next(it)
    l_ref = next(it) if want_lse else None
    c = pl.program_id(0)
    hg = pl.program_id(1)
    nh = len(kv_off)
    group_scale = jnp.float32(1.0)
    for n in range(1, 4):
        group_scale = jnp.where(hg == n, jnp.float32(head_step ** n), group_scale)
    qi = lax.broadcasted_iota(jnp.int32, (BLK, 2 * BLK), 0)
    kc = lax.broadcasted_iota(jnp.int32, (BLK, 2 * BLK), 1)
    steps = BLK + qi - kc
    in_band = jnp.logical_and(steps >= 0, steps <= BLK)
    in_band_first = jnp.logical_and(in_band, jnp.logical_or(kc >= BLK, c > 0))
    dist = steps.astype(F32) * float(dil)

    def rows(start):
        return pl.ds(start, BLK, stride=dil) if dil > 1 else pl.ds(start, BLK)

    for res in range(dil):
        for w in range(sub):
            cur = rows(res + BLK * w * dil)
            q_all = q_ref[cur, :]
            if w == 0:
                k_prev, v_prev = kp_ref[rows(res), :], vp_ref[rows(res), :]
                valid = in_band_first
            else:
                prv = rows(res + BLK * (w - 1) * dil)
                k_prev, v_prev = kc_ref[prv, :], vc_ref[prv, :]
                valid = in_band
            k_all = jnp.concatenate([k_prev, kc_ref[cur, :]], axis=0).astype(BF16)
            v_all = jnp.concatenate([v_prev, vc_ref[cur, :]], axis=0).astype(BF16)
            if has_prev:
                op_all, lp_all = op_ref[cur, :], lp_ref[cur, :]
            o_parts, l_parts = [], []
            for h in range(nh):
                hs = slice(h * HD, (h + 1) * HD)
                ks = slice(kv_off[h], kv_off[h] + HD)
                slope = _slope(h) * group_scale
                q = q_all[:, hs].astype(BF16)
                s = jnp.where(valid, _nt(q, k_all[:, ks]) * SCALE - slope * dist, NEG)
                m = jnp.max(s, axis=-1, keepdims=True)
                p = jnp.exp(s - m)
                l = jnp.sum(p, axis=-1, keepdims=True)
                if has_sink:
                    l = l + jnp.exp(sink_ref[h] - m)
                o = _mm(p.astype(BF16), v_all[:, ks])
                o = o / l
                lse = m + jnp.log(l)
                if has_prev:
                    lse_prev = lp_all[:, hs]
                    lse_new = jnp.logaddexp(lse_prev, lse)
                    o = op_all[:, hs] * jnp.exp(lse_prev - lse_new) + o * jnp.exp(lse - lse_new)
                    lse = lse_new
                o_parts.append(o)
                l_parts.append(jnp.broadcast_to(lse, (BLK, HD)))
            o_ref[cur, :] = jnp.concatenate(o_parts, axis=1)
            if want_lse:
                l_ref[cur, :] = jnp.concatenate(l_parts, axis=1)


def _band_prompt(proj, dil, sub, q_col, k_col, v_col, nh, kv_w, kv_off, head_step=1.0, sinks=None, prev=None,
                 want_lse=False):
    s, w = proj.shape
    chunk = BLK * dil * sub
    tail = BLK * dil
    qw = nh * HD
    ngroups = 512 // qw
    kv_groups = 1 if kv_w == 128 and nh == 8 else ngroups
    kcol = (lambda g: k_col // kv_w + g) if kv_groups > 1 else (lambda g: k_col // kv_w)
    vcol = (lambda g: v_col // kv_w + g) if kv_groups > 1 else (lambda g: v_col // kv_w)
    qspec = pl.BlockSpec((chunk, qw), lambda c, g: (c, q_col // qw + g))
    kc = pl.BlockSpec((chunk, kv_w), lambda c, g: (c, kcol(g)))
    kp = pl.BlockSpec((tail, kv_w), lambda c, g: (jnp.maximum(c * sub - 1, 0), kcol(g)))
    vc = pl.BlockSpec((chunk, kv_w), lambda c, g: (c, vcol(g)))
    vp = pl.BlockSpec((tail, kv_w), lambda c, g: (jnp.maximum(c * sub - 1, 0), vcol(g)))
    ospec = pl.BlockSpec((chunk, qw), lambda c, g: (c, g))
    ins = [proj, proj, proj, proj, proj]
    specs = [qspec, kc, kp, vc, vp]
    if sinks is not None:
        ins.append(sinks)
        specs.append(pl.BlockSpec(memory_space=pltpu.SMEM))
    if prev is not None:
        ins += [prev[0], prev[1]]
        specs += [ospec, ospec]
    oshape = jax.ShapeDtypeStruct((s, 512), F32)
    return pl.pallas_call(
        functools.partial(_band_prompt_kernel, dil=dil, sub=sub, kv_off=tuple(kv_off), head_step=head_step,
                          has_sink=sinks is not None, has_prev=prev is not None, want_lse=want_lse),
        grid=(s // chunk, ngroups),
        in_specs=specs,
        out_specs=[ospec, ospec] if want_lse else ospec,
        out_shape=[oshape, oshape] if want_lse else oshape,
        compiler_params=_params(("parallel", "parallel")),
        name=f"band_prompt_d{dil}",
    )(*ins)


def _col_consts(values, rows):
    r = lax.broadcasted_iota(jnp.int32, (rows, 1), 0)
    col = jnp.zeros((rows, 1), F32)
    for h, v in enumerate(values):
        col = jnp.where(r == h, v, col)
    return col


def _sample_even_kernel(ps_ref, a1_ref, a2_ref, a3_ref, cb_ref, fold_ref, sink_ref, oa_ref, ob_ref, *, t_len):
    nh = 8
    nr = t_len * nh
    row = lax.broadcasted_iota(jnp.int32, (nr, 1), 0)
    t_col = row // nh
    h_col = row % nh
    hm = (lax.broadcasted_iota(jnp.int32, (nr, nh * HD), 1) // HD == h_col).astype(F32)
    slope = jnp.zeros((nr, 1), F32)
    sink_col = jnp.zeros((nr, 1), F32)
    for h in range(nh):
        slope = jnp.where(h_col == h, _slope(h), slope)
        sink_col = jnp.where(h_col == h, sink_ref[h], sink_col)

    def q_rows(col):
        return jnp.concatenate([jnp.broadcast_to(ps_ref[t:t + 1, col:col + 512], (nh, 512))
                                for t in range(t_len)], axis=0) * hm

    def attend(qk, q_new, kt_ref_slice, vt_ref_slice, k_col, v_col, width, dil, sinks):
        kt = kt_ref_slice.astype(BF16)
        length = kt.shape[1]
        dd = length + t_col - lax.broadcasted_iota(jnp.int32, (nr, length), 1)
        valid = jnp.logical_and((dd & (dil - 1)) == 0, dd <= dil * BLK)
        s = jnp.where(valid, _mm(qk, kt) * SCALE - slope * dd.astype(F32), NEG)
        m = jnp.max(s, axis=-1, keepdims=True)
        new = []
        for tp in range(t_len):
            dn = t_col - tp
            ok = jnp.logical_and(dn >= 0, (dn & (dil - 1)) == 0)
            sc = jnp.sum(q_new * ps_ref[tp:tp + 1, k_col:k_col + width], axis=-1, keepdims=True) * SCALE
            sc = jnp.where(ok, sc - slope * dn.astype(F32), NEG)
            new.append(sc)
            m = jnp.maximum(m, sc)
        p = jnp.exp(s - m)
        l = jnp.sum(p, axis=-1, keepdims=True)
        o = _nt(p.astype(BF16), vt_ref_slice.astype(BF16))
        for tp in range(t_len):
            pn = jnp.exp(new[tp] - m)
            l = l + pn
            o = o + pn * ps_ref[tp:tp + 1, v_col:v_col + width]
        if sinks:
            l = l + jnp.exp(sink_col - m)
        return o / l, m + jnp.log(l)

    def head_rows(o_wide):
        return jnp.sum((o_wide * hm).reshape(t_len, nh, nh * HD), axis=1)

    outs, lses = [], []
    for g, (buf, (_, dil)) in enumerate(zip((a1_ref, a2_ref, a3_ref), A_PATTERNS)):
        q = q_rows(g * 512)
        o, lse = attend(q.astype(BF16), q, buf[0:512, :], buf[512:1024, :],
                        1536 + g * 512, 3072 + g * 512, 512, dil, False)
        outs.append(o)
        lses.append(lse)
    mx = jnp.maximum(jnp.maximum(lses[0], lses[1]), lses[2])
    es = [jnp.exp(l_ - mx) for l_ in lses]
    merged = (es[0] * outs[0] + es[1] * outs[1] + es[2] * outs[2]) / (es[0] + es[1] + es[2])
    oa_ref[...] = head_rows(merged)

    qf = _mm(q_rows(4608).astype(BF16), fold_ref[...])
    o, _ = attend(qf.astype(BF16), qf, cb_ref[0:128, :], cb_ref[128:256, :], 5120, 5248, 128, 1, True)
    o_sw = pltpu.roll(o, HD, 1)
    o_sel = jnp.where((h_col // 4) == (h_col % 2), o, o_sw)
    ob_ref[...] = head_rows(jnp.concatenate([o_sel] * 4, axis=1))


def _fold_matrix():
    c = np.arange(512)[:, None]
    l = np.arange(128)[None, :]
    return jnp.asarray(((c % HD == l % HD) & (l // HD == (c // HD) // 4)).astype(np.float32), BF16)


def _sample_even(ps, a1, a2, a3, cb, layer, sinks, t_len=4):
    nl, n = a1.shape[0], a1.shape[1]
    ps3 = ps.reshape(n, t_len, ps.shape[1])

    def rows_last(c):
        return jnp.transpose(c, (0, 1, 3, 4, 5, 2)).reshape(nl * n, -1, c.shape[2])

    a1v, a2v, a3v, cbv = rows_last(a1), rows_last(a2), rows_last(a3), rows_last(cb)
    b3 = lambda arr: pl.BlockSpec((None,) + arr.shape[1:], lambda i: (layer * n + i, 0, 0))
    ospec = pl.BlockSpec((None, t_len, 512), lambda i: (i, 0, 0))
    oa, ob = pl.pallas_call(
        functools.partial(_sample_even_kernel, t_len=t_len),
        grid=(n,),
        in_specs=[pl.BlockSpec((None, t_len, ps.shape[1]), lambda i: (i, 0, 0)),
                  b3(a1v), b3(a2v), b3(a3v), b3(cbv),
                  pl.BlockSpec((512, 128), lambda i: (0, 0)),
                  pl.BlockSpec(memory_space=pltpu.SMEM)],
        out_specs=[ospec, ospec],
        out_shape=[jax.ShapeDtypeStruct((n, t_len, 512), F32)] * 2,
        compiler_params=_params(("parallel",)),
        name="sample_even_attn",
    )(ps3, a1v, a2v, a3v, cbv, _fold_matrix(), sinks)
    return oa.reshape(n * t_len, 512), ob.reshape(n * t_len, 512)


C_GROUP = C_HEADS // C_KV
D_GROUP = D_HEADS // D_KV


def _lambda(lamv, lambda_init):
    s1 = jnp.sum(lamv[0:1, :] * lamv[1:2, :], axis=-1, keepdims=True)
    s2 = jnp.sum(lamv[2:3, :] * lamv[3:4, :], axis=-1, keepdims=True)
    return jnp.exp(s1) - jnp.exp(s2) + lambda_init


def _sub_rms(o0, o1, lam, g, lambda_init):
    d = o0 - lam * o1
    return d * lax.rsqrt(jnp.mean(d * d, axis=-1, keepdims=True) + LN_EPS) * g * (1.0 - lambda_init)


def _softplus(z):
    return jnp.maximum(z, 0.0) + jnp.log(1.0 + jnp.exp2(jnp.abs(z) * (-math.log2(math.e))))


def _tri_matrix(n):
    u = (np.arange(n)[:, None] >= np.arange(n)[None, :]).astype(np.float32)
    return jnp.asarray(np.concatenate([u, u], axis=0), BF16)


def _rev_cumsum(sp, uu):
    hi = sp.astype(BF16)
    lo = (sp - hi.astype(F32)).astype(BF16)
    return _mm(jnp.concatenate([hi, lo], axis=1), uu)


def _matmul_t_kernel(wt_ref, x_ref, o_ref):
    o_ref[...] = _nt(wt_ref[...], x_ref[...].astype(BF16))


def _matmul_t(wt, x, tm=512):
    n, k = wt.shape
    m = x.shape[0]
    tm = min(tm, m)
    return pl.pallas_call(
        _matmul_t_kernel,
        grid=(m // tm,),
        in_specs=[pl.BlockSpec((n, k), lambda i: (0, 0)), pl.BlockSpec((tm, k), lambda i: (i, 0))],
        out_specs=pl.BlockSpec((n, tm), lambda i: (0, i)),
        out_shape=jax.ShapeDtypeStruct((n, m), F32),
        compiler_params=_params(("parallel",)),
        name="proj_matmul_t",
    )(wt, x)


def _diff_prompt_kernel(q_ref, k_ref, vt_ref, lamv_ref, g_ref, o_ref, b0_ref, m_ref, l_ref, acc_ref, *, tq, lambda_init):
    hk = pl.program_id(0)
    i = pl.program_id(1)
    rows = C_GROUP * tq
    lane = lax.broadcasted_iota(jnp.int32, (1, rows), 1)
    slope = jnp.zeros((1, rows), F32)
    for g in range(C_GROUP):
        slope = jnp.where(lane // tq == g, _slope(g), slope)
    slope = slope * jnp.where(hk == 0, 1.0, 2.0 ** (-C_GROUP)).astype(F32)
    kl = lax.broadcasted_iota(jnp.int32, (tq, rows), 0)
    ql = lax.broadcasted_iota(jnp.int32, (tq, rows), 1) % tq
    b0_ref[...] = slope * (kl - ql).astype(F32)
    qs = []
    for m in range(2):
        parts = [q_ref[:, (g * 2 + m) * HD:(g * 2 + m + 1) * HD] for g in range(C_GROUP)]
        qs.append((jnp.concatenate(parts, axis=0) * SCALE).astype(BF16))
    m_ref[...] = jnp.full(m_ref.shape, NEG, F32)
    l_ref[...] = jnp.zeros_like(l_ref)
    acc_ref[...] = jnp.zeros_like(acc_ref)

    def blocks(js, diagonal):
        ks, vts, cjs = [], [], []
        for j in js:
            start = pl.multiple_of(j * tq, tq)
            ks.append(k_ref[pl.ds(start, tq), :].astype(BF16))
            vts.append(vt_ref[:, pl.ds(start, tq)].astype(BF16))
            cjs.append(-slope * ((i - j) * tq).astype(F32))
        for m in range(2):
            ss = []
            m_old = m_ref[m]
            m_new = m_old
            for k, cj in zip(ks, cjs):
                s = _nt(k[:, m * HD:(m + 1) * HD], qs[m]) + b0_ref[...]
                if diagonal:
                    s = jnp.where(kl <= ql, s, NEG)
                ss.append(s)
                m_new = jnp.maximum(m_new, jnp.max(s, axis=0, keepdims=True) + cj)
            alpha = jnp.exp(m_old - m_new)
            l = alpha * l_ref[m]
            acc = alpha * acc_ref[m]
            for s, vt, cj in zip(ss, vts, cjs):
                p = jnp.exp(s + (cj - m_new))
                l = l + jnp.sum(p, axis=0, keepdims=True)
                acc = acc + _mm(vt, p.astype(BF16))
            l_ref[m] = l
            acc_ref[m] = acc
            m_ref[m] = m_new

    @pl.when(i % 2 == 1)
    def _():
        blocks((0,), False)

    def off_diagonal(it, carry):
        j = i % 2 + 2 * it
        blocks((j, j + 1), False)
        return carry

    lax.fori_loop(0, i // 2, off_diagonal, 0)
    blocks((i,), True)
    lam = _lambda(lamv_ref[...], lambda_init)
    d = acc_ref[0] / l_ref[0] - lam * (acc_ref[1] / l_ref[1])
    d = d * lax.rsqrt(jnp.mean(d * d, axis=0, keepdims=True) + LN_EPS) * g_ref[...] * (1.0 - lambda_init)
    for g in range(C_GROUP):
        o_ref[:, g * 2 * HD:(g + 1) * 2 * HD] = d[:, g * tq:(g + 1) * tq].T


def _diff_prompt(proj, vt, lamv, subln_g, lambda_init, tq=256):
    s = proj.shape[0]
    rows = C_GROUP * tq
    return pl.pallas_call(
        functools.partial(_diff_prompt_kernel, tq=tq, lambda_init=lambda_init),
        grid=(C_KV, s // tq),
        in_specs=[pl.BlockSpec((tq, 512), lambda h, i: (i, h)),
                  pl.BlockSpec((s, 128), lambda h, i: (0, 8 + h)),
                  pl.BlockSpec((2 * HD, s), lambda h, i: (h, 0)),
                  pl.BlockSpec((4, HD), lambda h, i: (0, 0)),
                  pl.BlockSpec((2 * HD, 1), lambda h, i: (0, 0))],
        out_specs=pl.BlockSpec((tq, 512), lambda h, i: (i, h)),
        out_shape=jax.ShapeDtypeStruct((s, C_HEADS * 2 * HD), F32),
        scratch_shapes=[pltpu.VMEM((tq, rows), F32), pltpu.VMEM((2, 1, rows), F32), pltpu.VMEM((2, 1, rows), F32),
                        pltpu.VMEM((2, 2 * HD, rows), F32)],
        compiler_params=_params(("parallel", "arbitrary")),
        name="diff_attn_prompt",
    )(proj, proj, vt, lamv, subln_g.reshape(2 * HD, 1))


def _tri_matrix_t(n):
    u = (np.arange(n)[None, :] >= np.arange(n)[:, None]).astype(np.float32)
    return jnp.asarray(np.concatenate([u, u], axis=1), BF16)


def _sb_prompt_kernel(q_ref, k_ref, vt_ref, ut_ref, o_ref, c_ref, acc_ref, *, tq):
    i = pl.program_id(1)
    rows = D_GROUP * tq
    kl = lax.broadcasted_iota(jnp.int32, (tq, rows), 0)
    ql = lax.broadcasted_iota(jnp.int32, (tq, rows), 1) % tq
    qs = []
    for hl in range(2):
        parts = [q_ref[:, (hl * D_GROUP + g) * HD:(hl * D_GROUP + g + 1) * HD] for g in range(D_GROUP)]
        qs.append((jnp.concatenate(parts, axis=0) * SCALE).astype(BF16))
    c_ref[...] = jnp.zeros_like(c_ref)
    acc_ref[...] = jnp.zeros_like(acc_ref)

    def blocks(js, diagonal):
        for hl in range(2):
            c = c_ref[hl]
            pv = None
            for j in js:
                start = pl.multiple_of(j * tq, tq)
                k = k_ref[pl.ds(start, tq), hl * HD:(hl + 1) * HD].astype(BF16)
                vt = vt_ref[hl * HD:(hl + 1) * HD, pl.ds(start, tq)].astype(BF16)
                z = _nt(k, qs[hl])
                sp = _softplus(z)
                if diagonal:
                    sp = jnp.where(kl < ql, sp, 0.0)
                hi = sp.astype(BF16)
                lo = (sp - hi.astype(F32)).astype(BF16)
                tl = _mm(ut_ref[...], jnp.concatenate([hi, lo], axis=0))
                e = z - tl - c
                if diagonal:
                    e = jnp.where(kl < ql, e, NEG)
                term = _mm(vt, jnp.exp(e).astype(BF16))
                pv = term if pv is None else pv + term
                c = c + tl[0:1, :]
            acc_ref[hl * HD:(hl + 1) * HD, :] += pv
            c_ref[hl] = c

    blocks((i,), True)

    @pl.when(i % 2 == 1)
    def _():
        blocks((i - 1,), False)

    top = i - 1 - i % 2

    def older(it, carry):
        blocks((top - 2 * it, top - 2 * it - 1), False)
        return carry

    lax.fori_loop(0, i // 2, older, 0)
    for g in range(D_GROUP):
        t = acc_ref[:, g * tq:(g + 1) * tq].T
        for hl in range(2):
            o_ref[:, (hl * D_GROUP + g) * HD:(hl * D_GROUP + g + 1) * HD] = t[:, hl * HD:(hl + 1) * HD]


def _sb_prompt(proj, vt, tq=256):
    s = proj.shape[0]
    rows = D_GROUP * tq
    return pl.pallas_call(
        functools.partial(_sb_prompt_kernel, tq=tq),
        grid=(D_KV // 2, s // tq),
        in_specs=[pl.BlockSpec((tq, 512), lambda h, i: (i, 3 + h)),
                  pl.BlockSpec((s, 128), lambda h, i: (0, 20 + h)),
                  pl.BlockSpec((2 * HD, s), lambda h, i: (2 + h, 0)),
                  pl.BlockSpec((tq, 2 * tq), lambda h, i: (0, 0))],
        out_specs=pl.BlockSpec((tq, 512), lambda h, i: (i, h)),
        out_shape=jax.ShapeDtypeStruct((s, D_HEADS * HD), F32),
        scratch_shapes=[pltpu.VMEM((2, 1, rows), F32), pltpu.VMEM((2 * HD, rows), F32)],
        compiler_params=_params(("parallel", "arbitrary")),
        name="sb_attn_prompt",
    )(proj, proj, vt, _tri_matrix_t(tq))


PAGES_PER_STEP = 16
QROWS = 64


def _sample_odd_kernel(pt_ref, qc_ref, qd_ref, ps_ref, *refs, npages, t_len, past_len, lambda_init):
    pp = PAGES_PER_STEP
    hrows = QROWS // C_KV
    pools = refs[0:4]
    uu_ref, lamv_ref, g_ref, oc_ref, od_ref = refs[4:9]
    bufs = refs[9:13]
    sem, m_ref, l_ref, accc_ref, c_ref, accd_ref = refs[13:]
    b = pl.program_id(0)
    j = pl.program_id(1)
    nsteps = pl.num_programs(1)
    slot = j % 2

    def page_copies(bb, jj, sl):
        out = []
        for i in range(pp):
            page = pt_ref[bb, npages - 1 - (jj * pp + i)]
            for pool, buf in zip(pools, bufs):
                out.append(pltpu.make_async_copy(pool.at[page], buf.at[sl, i], sem.at[sl]))
        return out

    def start_all(copies):
        for n, cp in enumerate(copies):
            cp.start(priority=n % 2)

    @pl.when(jnp.logical_and(b == 0, j == 0))
    def _():
        start_all(page_copies(0, 0, 0))
        start_all(page_copies(0, 1, 1))

    for cp in page_copies(b, j, slot):
        cp.wait()
    kc_refs = [bufs[0].at[slot, i] for i in range(pp)]
    vc_refs = [bufs[1].at[slot, i] for i in range(pp)]
    kd_refs = [bufs[2].at[slot, i] for i in range(pp)]
    vd_refs = [bufs[3].at[slot, i] for i in range(pp)]
    r = lax.broadcasted_iota(jnp.int32, (QROWS, 1), 0)
    tc_col = (r // C_GROUP) % t_len
    td_col = (r // D_GROUP) % t_len
    head_c = (r // (2 * t_len * C_GROUP)) * C_GROUP + r % C_GROUP
    slope = jnp.zeros((QROWS, 1), F32)
    for h in range(C_HEADS):
        slope = jnp.where(head_c == h, _slope(h), slope)
    qc = qc_ref[...] * SCALE
    qd = qd_ref[...] * SCALE

    @pl.when(j == 0)
    def _():
        def per_head(tp, cols):
            return jnp.concatenate(
                [jnp.broadcast_to(ps_ref[tp:tp + 1, cols + hk * 128:cols + (hk + 1) * 128], (hrows, 128))
                 for hk in range(C_KV)], axis=0)

        scs = []
        for tp in range(t_len):
            sc = jnp.sum(qc * per_head(tp, 1024), axis=-1, keepdims=True)
            sc = sc - slope * (tc_col - tp).astype(F32)
            scs.append(jnp.where(tp <= tc_col, sc, NEG))
        m = scs[0]
        for sc in scs[1:]:
            m = jnp.maximum(m, sc)
        l = jnp.zeros((QROWS, 1), F32)
        acc = jnp.zeros((QROWS, 128), F32)
        for tp in range(t_len):
            p = jnp.exp(scs[tp] - m)
            l = l + p
            acc = acc + p * per_head(tp, 1280)
        m_ref[...] = m
        l_ref[...] = l
        accc_ref[...] = acc
        c = jnp.zeros((QROWS, 1), F32)
        acc = jnp.zeros((QROWS, 256), F32)
        for tp in range(t_len - 1, -1, -1):
            z = jnp.sum(qd * ps_ref[tp:tp + 1, 2560:2816], axis=-1, keepdims=True)
            earlier = tp < td_col
            sp = jnp.where(earlier, _softplus(z), 0.0)
            a = jnp.exp(jnp.where(earlier, z - sp - c, NEG))
            acc = acc + a * ps_ref[tp:tp + 1, 2816:3072]
            c = c + sp
        c_ref[...] = c
        accd_ref[...] = acc

    qcb = qc.astype(BF16)
    qpos = (past_len + tc_col).astype(F32)
    col = lax.broadcasted_iota(jnp.int32, (1, PAGE * C_KV), 1)
    own_head = (col % C_KV) == (r // hrows)
    ckey = col // C_KV
    ss = []
    for i in range(pp):
        page = npages - 1 - (j * pp + i)
        kpos = (page * PAGE + ckey).astype(F32)
        sc = _nt(qcb, kc_refs[i][...].astype(BF16))
        ss.append(jnp.where(own_head, sc - slope * (qpos - kpos), NEG))
    s = jnp.concatenate(ss, axis=1)
    m_old = m_ref[...]
    m_new = jnp.maximum(m_old, jnp.max(s, axis=-1, keepdims=True))
    alpha = jnp.exp(m_old - m_new)
    p = jnp.exp(s - m_new)
    l_ref[...] = alpha * l_ref[...] + jnp.sum(p, axis=-1, keepdims=True)
    p = p.astype(BF16)
    w = PAGE * C_KV
    pv = _mm(p[:, 0:w], vc_refs[0][...].astype(BF16))
    for i in range(1, pp):
        pv = pv + _mm(p[:, i * w:(i + 1) * w], vc_refs[i][...].astype(BF16))
    accc_ref[...] = alpha * accc_ref[...] + pv
    m_ref[...] = m_new

    qdb = qd.astype(BF16)
    kd_all = jnp.concatenate([kd_refs[i][...].astype(BF16) for i in range(pp)], axis=1)
    z = _mm(qdb, kd_all)
    sp = _softplus(z)
    hi = sp.astype(BF16)
    lo = (sp - hi.astype(F32)).astype(BF16)
    stacked = jnp.concatenate(
        [jnp.concatenate([hi[:, i * PAGE:(i + 1) * PAGE], lo[:, i * PAGE:(i + 1) * PAGE]], axis=1)
         for i in range(pp)], axis=0)
    tl_all = _mm(stacked, uu_ref[...])
    c = c_ref[...]
    acc = accd_ref[...]
    for i in range(pp):
        tl = tl_all[i * QROWS:(i + 1) * QROWS]
        a = jnp.exp(z[:, i * PAGE:(i + 1) * PAGE] - tl - c)
        acc = acc + _nt(a.astype(BF16), vd_refs[i][...].astype(BF16))
        c = c + tl[:, 0:1]
    c_ref[...] = c
    accd_ref[...] = acc

    two_ahead = jnp.logical_not(jnp.logical_and(b == pl.num_programs(0) - 1, j >= nsteps - 2))

    @pl.when(two_ahead)
    def _():
        wrap = j + 2 >= nsteps
        start_all(page_copies(jnp.where(wrap, b + 1, b), jnp.where(wrap, j + 2 - nsteps, j + 2), slot))

    @pl.when(j == pl.num_programs(1) - 1)
    def _():
        lam = _lambda(lamv_ref[...], lambda_init)
        o = accc_ref[...] / l_ref[...]
        half = t_len * C_GROUP
        for hk in range(C_KV):
            o0 = o[(hk * 2) * half:(hk * 2 + 1) * half, :]
            o1 = o[(hk * 2 + 1) * half:(hk * 2 + 2) * half, :]
            oc_ref[hk * half:(hk + 1) * half, :] = _sub_rms(o0, o1, lam, g_ref[...], lambda_init)
        per = t_len * D_GROUP
        accd = accd_ref[...]
        for hk in range(D_KV):
            od_ref[hk * per:(hk + 1) * per, :] = accd[hk * per:(hk + 1) * per, hk * HD:(hk + 1) * HD]


def _sample_odd(ps, pools, page_base, page_table, lamv, subln_g, lambda_init, t_len=4):
    n, npages = page_table.shape
    pp = PAGES_PER_STEP
    ps3 = ps.reshape(n, t_len, ps.shape[1])
    eye2 = jnp.eye(2, dtype=F32)
    eye4 = jnp.eye(4, dtype=F32)
    qc = ps3[:, :, 0:1024].reshape(n, t_len, C_KV, C_GROUP, 2, HD).transpose(0, 2, 4, 1, 3, 5)
    qc = (qc[:, :, :, :, :, None, :] * eye2[None, None, :, None, None, :, None]).reshape(n, QROWS, 128)
    qd = ps3[:, :, 1536:2560].reshape(n, t_len, D_KV, D_GROUP, HD).transpose(0, 2, 1, 3, 4)
    qd = (qd[:, :, :, :, None, :] * eye4[None, :, None, None, :, None]).reshape(n, QROWS, 256)
    pt = page_table + page_base

    nsteps = npages // pp
    assert nsteps % 2 == 0, "the page double buffer keys its slot on the step parity"
    per_b = lambda w, rws: pl.BlockSpec((None, rws, w), lambda b, j, pt_ref: (b, 0, 0))
    fixed = lambda shape: pl.BlockSpec(shape, lambda b, j, pt_ref: (0, 0))
    in_specs = [per_b(128, QROWS), per_b(256, QROWS), per_b(ps.shape[1], t_len)]
    in_specs += [pl.BlockSpec(memory_space=pl.ANY)] * 4
    in_specs += [fixed((2 * PAGE, PAGE)), fixed((4, HD)), fixed((1, 2 * HD))]
    ins = [qc, qd, ps3, *pools, _tri_matrix(PAGE), lamv, subln_g.reshape(1, 2 * HD)]
    page_buf = pltpu.VMEM((2, pp, 256, PAGE), F32)
    oc, od = pl.pallas_call(
        functools.partial(_sample_odd_kernel, npages=npages, t_len=t_len, past_len=npages * PAGE,
                          lambda_init=lambda_init),
        grid_spec=pltpu.PrefetchScalarGridSpec(
            num_scalar_prefetch=1,
            grid=(n, nsteps),
            in_specs=in_specs,
            out_specs=[per_b(2 * HD, C_KV * t_len * C_GROUP), per_b(HD, D_KV * t_len * D_GROUP)],
            scratch_shapes=[page_buf, page_buf, page_buf, page_buf, pltpu.SemaphoreType.DMA((2,)),
                            pltpu.VMEM((QROWS, 1), F32), pltpu.VMEM((QROWS, 1), F32), pltpu.VMEM((QROWS, 128), F32),
                            pltpu.VMEM((QROWS, 1), F32), pltpu.VMEM((QROWS, 256), F32)]),
        out_shape=[jax.ShapeDtypeStruct((n, C_KV * t_len * C_GROUP, 2 * HD), F32),
                   jax.ShapeDtypeStruct((n, D_KV * t_len * D_GROUP, HD), F32)],
        compiler_params=_params(("arbitrary", "arbitrary")),
        name="sample_odd_attn",
    )(pt, *ins)
    oc = oc.reshape(n, C_KV, t_len, C_GROUP, 2 * HD).transpose(0, 2, 1, 3, 4).reshape(n * t_len, C_HEADS * 2 * HD)
    od = od.reshape(n, D_KV, t_len, D_GROUP, HD).transpose(0, 2, 1, 3, 4).reshape(n * t_len, D_HEADS * HD)
    return oc, od


def kernel(x_prompt, x_sample, cache_a1, cache_a2, cache_a3, cache_b, cache_c_k, cache_c_v, cache_d_k, cache_d_v,
           state_conv, page_table, w_in_even, w_out_even, sinks_b, w_in_odd, w_out_odd,
           lam_q1, lam_k1, lam_q2, lam_k2, subln_g, w_ffn_a, conv_w, conv_b, w_ffn_g, w_ffn_down,
           ln_mix_g, ln_mix_b, ln_ffn_g, ln_ffn_b):
    bsz, seq, d = x_prompt.shape
    nb, t_len, _ = x_sample.shape
    assert bsz == 1, "the prompt group is one sequence"
    xp = x_prompt.reshape(seq, d)
    xs = x_sample.reshape(nb * t_len, d)
    b_off = [(h // (B_HEADS // B_KV)) * HD for h in range(B_HEADS)]
    n_phys = cache_c_k.shape[1]
    pools = [c.reshape(c.shape[0] * n_phys, PAGE * C_KV, 2 * HD) for c in (cache_c_k, cache_c_v)]
    pools += [jnp.transpose(c, (0, 1, 3, 4, 2)).reshape(c.shape[0] * n_phys, D_KV * HD, PAGE)
              for c in (cache_d_k, cache_d_v)]
    even_p, even_s, odd_p, odd_s, conv_p, conv_s = [], [], [], [], [], []
    for layer in range(DEPTH):
        i = layer // 2
        if layer % 2 == 0:
            w_in = jnp.pad(w_in_even[i].astype(BF16), ((0, 0), (0, EVEN_PAD - EVEN_IN)))
            w_out = w_out_even[i].astype(BF16)
            pp = _matmul(xp, w_in, 512, 1408)
            ps = _matmul(xs, w_in, 512, 1408)
            prev = None
            for g, (_, dil) in enumerate(A_PATTERNS):
                last = g == len(A_PATTERNS) - 1
                res = _band_prompt(pp, dil, 4 if dil == 1 else 1, g * 512, 1536 + g * 512, 3072 + g * 512,
                                   2, 128, (0, HD), head_step=0.25, prev=prev, want_lse=not last)
                prev = None if last else res
            oa_p = res
            ob_p = _band_prompt(pp, 1, 4, 4608, 5120, 5248, B_HEADS, 128, b_off, sinks=sinks_b[i].reshape(B_HEADS))
            oa_s, ob_s = _sample_even(ps, cache_a1, cache_a2, cache_a3, cache_b, i, sinks_b[i].reshape(B_HEADS), t_len)
            w1, w2 = w_out[:512], w_out[512:]
            rows_p, rows_s = [], []
            for g, (win, _) in enumerate(A_PATTERNS):
                w = min(win, seq)
                kv = jnp.stack([pp[seq - w:, 1536 + g * 512:2048 + g * 512].reshape(w, A_HEADS, HD),
                                pp[seq - w:, 3072 + g * 512:3584 + g * 512].reshape(w, A_HEADS, HD)], axis=1)
                rows_p.append(kv[None])
                rows_s.append(jnp.stack([ps[:, 1536 + g * 512:2048 + g * 512].reshape(nb, t_len, A_HEADS, HD),
                                         ps[:, 3072 + g * 512:3584 + g * 512].reshape(nb, t_len, A_HEADS, HD)], axis=2))
            w = min(128, seq)
            rows_p.append(jnp.stack([pp[seq - w:, 5120:5248].reshape(w, B_KV, HD),
                                     pp[seq - w:, 5248:5376].reshape(w, B_KV, HD)], axis=1)[None])
            rows_s.append(jnp.stack([ps[:, 5120:5248].reshape(nb, t_len, B_KV, HD),
                                     ps[:, 5248:5376].reshape(nb, t_len, B_KV, HD)], axis=2))
            even_p.append(rows_p)
            even_s.append(rows_s)
            mix_p, mix_s = (oa_p, ob_p), (oa_s, ob_s)
        else:
            lambda_init = 0.8 - 0.6 * math.exp(-0.3 * layer)
            lamv = jnp.stack([lam_q1[i], lam_k1[i], lam_q2[i], lam_k2[i]])
            w_in = w_in_odd[i].astype(BF16)
            w_out = w_out_odd[i].astype(BF16)
            pp = _matmul(xp, w_in, 512, 1024)
            ps = _matmul(xs, w_in, 512, 1024)
            wvt = jnp.concatenate([w_in[:, 1280:1536], w_in[:, 2816:3072]], axis=1).T
            vt = _matmul_t(wvt, xp)
            oc_p = _diff_prompt(pp, vt, lamv, subln_g[i], lambda_init)
            od_p = _sb_prompt(pp, vt)
            oc_s, od_s = _sample_odd(ps, pools, i * n_phys, page_table, lamv, subln_g[i], lambda_init, t_len)
            w1, w2 = w_out[:1024], w_out[1024:]
            odd_p.append((pp[:, 1024:1280].reshape(1, seq, C_KV, 2 * HD), pp[:, 1280:1536].reshape(1, seq, C_KV, 2 * HD),
                          pp[:, 2560:2816].reshape(1, seq, D_KV, HD), pp[:, 2816:3072].reshape(1, seq, D_KV, HD)))
            odd_s.append((ps[:, 1024:1280].reshape(nb, t_len, C_KV, 2 * HD), ps[:, 1280:1536].reshape(nb, t_len, C_KV, 2 * HD),
                          ps[:, 2560:2816].reshape(nb, t_len, D_KV, HD), ps[:, 2816:3072].reshape(nb, t_len, D_KV, HD)))
            mix_p, mix_s = (oc_p, od_p), (oc_s, od_s)
        xp = _outproj_ln(mix_p[0], mix_p[1], w1, w2, xp, ln_mix_g[layer], ln_mix_b[layer])
        xs = _outproj_ln(mix_s[0], mix_s[1], w1, w2, xs, ln_mix_g[layer], ln_mix_b[layer])
        wa, wg, wd = w_ffn_a[layer].astype(BF16), w_ffn_g[layer].astype(BF16), w_ffn_down[layer].astype(BF16)
        ffn_args = (wa, wg, wd, conv_w[layer], conv_b[layer], ln_ffn_g[layer], ln_ffn_b[layer])
        xp, tail_p = _ffn(xp, *ffn_args)
        st = state_conv[layer]
        p1 = jnp.concatenate([st[:, 1:2], jnp.zeros((nb, t_len - 1, D_FF), F32)], axis=1).reshape(nb * t_len, D_FF)
        p2 = jnp.concatenate([st, jnp.zeros((nb, t_len - 2, D_FF), F32)], axis=1).reshape(nb * t_len, D_FF)
        xs, tail_s = _ffn(xs, *ffn_args, prev=(p1, p2), t_len=t_len)
        conv_p.append(tail_p[-2:][None])
        conv_s.append(tail_s.reshape(nb, t_len, D_FF)[:, t_len - 2:])
    stack = lambda per_layer, j: jnp.stack([entry[j] for entry in per_layer])
    outs = [xp.reshape(1, seq, d), xs.reshape(nb, t_len, d)]
    for j in range(4):
        outs += [stack(even_p, j), stack(even_s, j)]
    for j in range(4):
        outs += [stack(odd_p, j), stack(odd_s, j)]
    outs += [jnp.stack(conv_p), jnp.stack(conv_s)]
    return tuple(outs)
```
